```python
import math
import jax
import jax.numpy as jnp
from jax import lax
import numpy as np

D_MODEL = 1024
BATCH = 8
SEQ = 4096
DEPTH = 4

CHUNK = 64
CONV_K = 4
GDN_HEADS = 8
GDN_DK = 128
GDN_DV = 128
SSD_INNER = D_MODEL
SSD_HEADDIM = 64
SSD_HEADS = SSD_INNER // SSD_HEADDIM
SSD_GROUPS = 4
SSD_HPG = SSD_HEADS // SSD_GROUPS
SSD_STATE = 128
RET_HEADS = 4
RET_DK = 128
RET_DV = 256
ROPE_BASE = 10000.0
N_BRANCH = 3
MEM_LEN = 256
XA_HEADS = 4
XA_DH = D_MODEL // XA_HEADS
N_EXPERTS = 16
N_EXPERT_GROUPS = 4
EXPERTS_PER_GROUP = N_EXPERTS // N_EXPERT_GROUPS
TOP_K = 2
D_EXPERT = 512
MOE_BLOCK = 128
DN_ALPHA = (2 * DEPTH) ** 0.25
DN_BETA = (8 * DEPTH) ** -0.25
LN_EPS = 1e-5
RMS_EPS = 1e-6

GDN_QKV = 2 * GDN_HEADS * GDN_DK + GDN_HEADS * GDN_DV
SSD_XBC = SSD_INNER + 2 * SSD_GROUPS * SSD_STATE
SPLIT_SIZES = (GDN_QKV, GDN_HEADS * GDN_DV, GDN_HEADS, GDN_HEADS,
               SSD_XBC, SSD_INNER, SSD_HEADS,
               RET_HEADS * RET_DK, RET_HEADS * RET_DK, RET_HEADS * RET_DV, RET_HEADS * RET_DV,
               N_BRANCH * D_MODEL)
D_IN = (GDN_QKV + GDN_HEADS * GDN_DV + 2 * GDN_HEADS + SSD_XBC + SSD_INNER + SSD_HEADS
        + 2 * RET_HEADS * RET_DK + 2 * RET_HEADS * RET_DV + N_BRANCH * D_MODEL)

kernel_name = "hybrid_gdn_ssd_retention_moe_encoder"


def layer_norm(x, g, b):
    xf = x.astype(jnp.float32)
    mu = jnp.mean(xf, -1, keepdims=True)
    var = jnp.mean(jnp.square(xf - mu), -1, keepdims=True)
    return ((xf - mu) * lax.rsqrt(var + LN_EPS) * g + b).astype(x.dtype)


def causal_depthwise_conv(x, w, b=None):
    y = lax.conv_general_dilated(x, w[:, None, :], window_strides=(1,), padding=[(CONV_K - 1, 0)],
                                 dimension_numbers=('NWC', 'WIO', 'NWC'),
                                 feature_group_count=x.shape[-1])
    return y if b is None else y + b


def rotary(x, cos, sin):
    x1, x2 = jnp.split(x, 2, axis=-1)
    c, s = cos[None, :, None, :], sin[None, :, None, :]
    return jnp.concatenate([x1 * c - x2 * s, x2 * c + x1 * s], axis=-1)


def split_columns(h):
    return jnp.split(h, np.cumsum(SPLIT_SIZES)[:-1].tolist(), axis=-1)


def chunked_decay_attention(q, k, v, log_a):
    f32 = jnp.float32
    bsz, seq, g, n = q.shape
    r, p = v.shape[-2:]
    nc = seq // CHUNK
    q = q.astype(f32).reshape(bsz, nc, CHUNK, g, n)
    k = k.astype(f32).reshape(bsz, nc, CHUNK, g, n)
    v = v.astype(f32).reshape(bsz, nc, CHUNK, g, r, p)
    cum = jnp.cumsum(log_a.astype(f32).reshape(bsz, nc, CHUNK, g, r), axis=2)
    idx = jnp.arange(CHUNK)
    causal = idx[:, None] >= idx[None, :]
    seg = cum[:, :, :, None] - cum[:, :, None, :]
    decay = jnp.exp(jnp.where(causal[:, :, None, None], seg, -jnp.inf))
    scores = jnp.einsum('bctgn,bcsgn->bctsg', q, k)
    y_intra = jnp.einsum('bctsgr,bcsgrp->bctgrp', scores[..., None] * decay, v)
    last = cum[:, :, -1]
    k_w = jnp.exp(last[:, :, None] - cum)
    chunk_kv = jnp.einsum('bcsgn,bcsgrp->bcgrnp', k, v * k_w[..., None])

    def step(state, inp):
        kv_c, a_c = inp
        return state * jnp.exp(a_c)[..., None, None] + kv_c, state

    _, prev = lax.scan(step, jnp.zeros((bsz, g, r, n, p), f32),
                       (jnp.moveaxis(chunk_kv, 1, 0), jnp.moveaxis(last, 1, 0)))
    prev = jnp.moveaxis(prev, 0, 1)
    y_inter = jnp.einsum('bctgn,bcgrnp->bctgrp', q, prev) * jnp.exp(cum)[..., None]
    return (y_intra + y_inter).reshape(bsz, seq, g, r, p)


def chunked_gated_delta_rule(q, k, v, log_a, beta):
    f32 = jnp.float32
    bsz, seq, h, dk = q.shape
    dv = v.shape[-1]
    nc = seq // CHUNK
    q, k, v = (t.astype(f32).reshape(bsz, nc, CHUNK, h, -1) for t in (q, k, v))
    beta = beta.astype(f32).reshape(bsz, nc, CHUNK, h)
    cum = jnp.cumsum(log_a.astype(f32).reshape(bsz, nc, CHUNK, h), axis=2)
    cum_h = jnp.swapaxes(cum, 2, 3)
    seg = cum_h[..., :, None] - cum_h[..., None, :]
    idx = jnp.arange(CHUNK)
    causal = idx[:, None] >= idx[None, :]
    strict = idx[:, None] > idx[None, :]
    decay = jnp.exp(jnp.where(causal, seg, -jnp.inf))
    kb = k * beta[..., None]
    vb = v * beta[..., None]
    a_strict = jnp.where(strict, jnp.einsum('bcthd,bcshd->bchts', kb, k) * decay, 0.0)
    eye = jnp.eye(CHUNK, dtype=f32)
    t_mat = lax.linalg.triangular_solve(a_strict + eye, jnp.broadcast_to(eye, a_strict.shape),
                                        left_side=True, lower=True, unit_diagonal=True)
    u = jnp.einsum('bchts,bcshd->bcthd', t_mat, vb)
    w = jnp.einsum('bchts,bcshd->bcthd', t_mat, kb * jnp.exp(cum)[..., None])
    attn = jnp.where(causal, jnp.einsum('bcthd,bcshd->bchts', q, k) * decay, 0.0)
    last = cum[:, :, -1]
    q_dec = q * jnp.exp(cum)[..., None]
    k_dec = k * jnp.exp(last[:, :, None] - cum)[..., None]

    def step(state, inp):
        u_c, w_c, q_c, k_c, attn_c, a_c = inp
        v_new = u_c - jnp.einsum('bthk,bhkv->bthv', w_c, state)
        o_c = (jnp.einsum('bthk,bhkv->bthv', q_c, state)
               + jnp.einsum('bhts,bshv->bthv', attn_c, v_new))
        state = state * jnp.exp(a_c)[..., None, None] + jnp.einsum('bshk,bshv->bhkv', k_c, v_new)
        return state, o_c

    xs = tuple(jnp.moveaxis(t, 1, 0) for t in (u, w, q_dec, k_dec, attn, last))
    _, o = lax.scan(step, jnp.zeros((bsz, h, dk, dv), f32), xs)
    return jnp.moveaxis(o, 0, 1).reshape(bsz, seq, h, dv)


def gated_deltanet_branch(qkv, z, a, b, conv_w, a_log, dt_bias, norm_w):
    bsz, seq, _ = qkv.shape
    dtype = qkv.dtype
    qkv = jax.nn.silu(causal_depthwise_conv(qkv, conv_w))
    q, k, v = jnp.split(qkv, [GDN_HEADS * GDN_DK, 2 * GDN_HEADS * GDN_DK], axis=-1)
    q = q.astype(jnp.float32).reshape(bsz, seq, GDN_HEADS, GDN_DK)
    k = k.astype(jnp.float32).reshape(bsz, seq, GDN_HEADS, GDN_DK)
    v = v.reshape(bsz, seq, GDN_HEADS, GDN_DV)
    q = q * lax.rsqrt(jnp.sum(jnp.square(q), -1, keepdims=True) + RMS_EPS) * GDN_DK ** -0.5
    k = k * lax.rsqrt(jnp.sum(jnp.square(k), -1, keepdims=True) + RMS_EPS)
    log_a = -jnp.exp(a_log.astype(jnp.float32)) * jax.nn.softplus(a.astype(jnp.float32) + dt_bias)
    beta = jax.nn.sigmoid(b.astype(jnp.float32))
    o = chunked_gated_delta_rule(q, k, v, log_a, beta)
    o = o * lax.rsqrt(jnp.mean(jnp.square(o), -1, keepdims=True) + RMS_EPS) * norm_w
    o = o * jax.nn.silu(z.astype(jnp.float32)).reshape(bsz, seq, GDN_HEADS, GDN_DV)
    return o.reshape(bsz, seq, GDN_HEADS * GDN_DV).astype(dtype)


def ssd_branch(xbc, z, dt_raw, conv_w, conv_b, a_log, dt_bias, d_skip, norm_w):
    bsz, seq, _ = xbc.shape
    dtype = xbc.dtype
    xbc = jax.nn.silu(causal_depthwise_conv(xbc, conv_w, conv_b))
    xs, bm, cm = jnp.split(xbc, [SSD_INNER, SSD_INNER + SSD_GROUPS * SSD_STATE], axis=-1)
    xs = xs.astype(jnp.float32).reshape(bsz, seq, SSD_GROUPS, SSD_HPG, SSD_HEADDIM)
    bm = bm.reshape(bsz, seq, SSD_GROUPS, SSD_STATE)
    cm = cm.reshape(bsz, seq, SSD_GROUPS, SSD_STATE)
    dt = jax.nn.softplus(dt_raw.astype(jnp.float32) + dt_bias).reshape(bsz, seq, SSD_GROUPS, SSD_HPG)
    log_a = -jnp.exp(a_log.astype(jnp.float32)).reshape(SSD_GROUPS, SSD_HPG) * dt
    y = chunked_decay_attention(cm, bm, xs * dt[..., None], log_a)
    y = y + d_skip.reshape(SSD_GROUPS, SSD_HPG, 1) * xs
    y = y.reshape(bsz, seq, SSD_INNER) * jax.nn.silu(z.astype(jnp.float32))
    y = y.reshape(bsz, seq, SSD_GROUPS, SSD_INNER // SSD_GROUPS)
    y = y * lax.rsqrt(jnp.mean(jnp.square(y), -1, keepdims=True) + RMS_EPS)
    return (y.reshape(bsz, seq, SSD_INNER) * norm_w).astype(dtype)


def retention_branch(q, k, v, g, cos, sin):
    bsz, seq, _ = q.shape
    dtype = q.dtype
    q = rotary(q.reshape(bsz, seq, RET_HEADS, RET_DK), cos, sin)
    k = rotary(k.reshape(bsz, seq, RET_HEADS, RET_DK), cos, sin) * RET_DK ** -0.5
    v = v.reshape(bsz, seq, RET_HEADS, 1, RET_DV)
    log_gamma = jnp.log1p(-jnp.exp2(-5.0 - jnp.arange(RET_HEADS, dtype=jnp.float32)))
    log_a = jnp.broadcast_to(log_gamma[:, None], (bsz, seq, RET_HEADS, 1))
    o = chunked_decay_attention(q, k, v, log_a)[:, :, :, 0]
    mu = jnp.mean(o, -1, keepdims=True)
    var = jnp.mean(jnp.square(o - mu), -1, keepdims=True)
    o = ((o - mu) * lax.rsqrt(var + LN_EPS)).reshape(bsz, seq, RET_HEADS * RET_DV)
    return (jax.nn.silu(g.astype(jnp.float32)) * o).astype(dtype)


def hybrid_mixer(x, w_in, gdn_conv_w, gdn_a_log, gdn_dt_bias, gdn_norm_w,
                 ssd_conv_w, ssd_conv_b, ssd_a_log, ssd_dt_bias, ssd_d, ssd_norm_w,
                 w_proj_gdn, w_proj_ssd, w_proj_ret, w_out, cos, sin):
    bsz, seq, _ = x.shape
    (gdn_qkv, gdn_z, gdn_a, gdn_b, ssd_xbc, ssd_z, ssd_dt,
     ret_q, ret_k, ret_v, ret_g, gate_logits) = split_columns(x @ w_in)
    y_gdn = gated_deltanet_branch(gdn_qkv, gdn_z, gdn_a, gdn_b, gdn_conv_w, gdn_a_log, gdn_dt_bias, gdn_norm_w)
    y_ssd = ssd_branch(ssd_xbc, ssd_z, ssd_dt, ssd_conv_w, ssd_conv_b, ssd_a_log, ssd_dt_bias, ssd_d, ssd_norm_w)
    y_ret = retention_branch(ret_q, ret_k, ret_v, ret_g, cos, sin)
    gates = jax.nn.sigmoid(gate_logits).reshape(bsz, seq, N_BRANCH, D_MODEL)
    merged = (gates[:, :, 0] * (y_gdn @ w_proj_gdn)
              + gates[:, :, 1] * (y_ssd @ w_proj_ssd)
              + gates[:, :, 2] * (y_ret @ w_proj_ret))
    return merged @ w_out


def memory_cross_attention(x, mem, w_q, w_k, w_v, w_o):
    bsz, seq, _ = x.shape
    q = (x @ w_q).reshape(bsz, seq, XA_HEADS, XA_DH)
    k = (mem @ w_k).reshape(bsz, -1, XA_HEADS, XA_DH)
    v = (mem @ w_v).reshape(bsz, -1, XA_HEADS, XA_DH)
    s = jnp.einsum('bshd,bmhd->bhsm', q, k).astype(jnp.float32) * XA_DH ** -0.5
    p = jax.nn.softmax(s, axis=-1).astype(v.dtype)
    o = jnp.einsum('bhsm,bmhd->bshd', p, v).reshape(bsz, seq, D_MODEL)
    return o @ w_o


def group_limited_route(xt, w_router, b_router):
    t = xt.shape[0]
    scores = jax.nn.sigmoid((xt @ w_router).astype(jnp.float32))
    sel = (scores + b_router).reshape(t, N_EXPERT_GROUPS, EXPERTS_PER_GROUP)
    group_score = jnp.sum(lax.top_k(sel, 2)[0], axis=-1)
    g_idx = jnp.argmax(group_score, axis=-1)
    in_group = jnp.take_along_axis(sel, g_idx[:, None, None], axis=1)[:, 0]
    _, local = lax.top_k(in_group, TOP_K)
    expert = g_idx[:, None] * EXPERTS_PER_GROUP + local
    w = jnp.take_along_axis(scores, expert, axis=1)
    return expert, w / jnp.sum(w, -1, keepdims=True)


def moe_ffn(x, w_router, b_router, w_gate, w_up, w_down):
    bsz, seq, d = x.shape
    xt = x.reshape(-1, d)
    t = xt.shape[0]
    expert, gate = group_limited_route(xt, w_router, b_router)
    flat_e = expert.reshape(-1)
    flat_tok = jnp.repeat(jnp.arange(t, dtype=jnp.int32), TOP_K)
    order = jnp.argsort(flat_e)
    se, stok, sg = flat_e[order], flat_tok[order], gate.reshape(-1)[order]
    counts = jnp.bincount(flat_e, length=N_EXPERTS)
    padded = (counts + MOE_BLOCK - 1) // MOE_BLOCK * MOE_BLOCK
    start = jnp.cumsum(counts) - counts
    pend = jnp.cumsum(padded)
    dest = (pend - padded)[se] + jnp.arange(t * TOP_K) - start[se]
    n_blocks = (t * TOP_K + N_EXPERTS * (MOE_BLOCK - 1) + MOE_BLOCK - 1) // MOE_BLOCK
    npad = n_blocks * MOE_BLOCK
    buf_tok = jnp.full((npad,), t, jnp.int32).at[dest].set(stok)
    buf_g = jnp.zeros((npad,), xt.dtype).at[dest].set(sg.astype(xt.dtype))
    block_expert = jnp.minimum(
        jnp.searchsorted(pend, jnp.arange(n_blocks) * MOE_BLOCK, side='right'), N_EXPERTS - 1)
    xpad = jnp.concatenate([xt, jnp.zeros((1, d), xt.dtype)], axis=0)
    xb = xpad[buf_tok].reshape(n_blocks, MOE_BLOCK, d)

    def expert_block(args):
        xblk, e = args
        hid = jax.nn.silu(xblk @ w_gate[e]) * (xblk @ w_up[e])
        return hid @ w_down[e]

    yb = lax.map(expert_block, (xb, block_expert)).reshape(npad, d)
    y = jax.ops.segment_sum(yb * buf_g[:, None], buf_tok, num_segments=t + 1)[:t]
    return y.reshape(bsz, seq, d)


def setup_inputs(seed: int = 0) -> dict:
    key = jax.random.key(seed)
    ks = iter(jax.random.split(key, 40))
    f32 = jnp.float32
    L = DEPTH

    def nrm(shape, scale):
        return jax.random.normal(next(ks), shape, f32) * scale

    def gain(shape):
        return 1.0 + nrm(shape, 0.02)

    def a_log(shape):
        return jnp.log(jax.random.uniform(next(ks), shape, f32, 1.0, 16.0))

    def dt_bias(shape):
        dt = jnp.exp(jax.random.uniform(next(ks), shape, f32, math.log(1e-3), math.log(1e-1)))
        return dt + jnp.log(-jnp.expm1(-dt))

    return {
        "x": nrm((BATCH, SEQ, D_MODEL), 1.0),
        "mem": nrm((BATCH, MEM_LEN, D_MODEL), 1.0),
        "mix_w_in": nrm((L, D_MODEL, D_IN), D_MODEL ** -0.5),
        "gdn_conv_w": nrm((L, CONV_K, GDN_QKV), CONV_K ** -0.5),
        "gdn_a_log": a_log((L, GDN_HEADS)),
        "gdn_dt_bias": dt_bias((L, GDN_HEADS)),
        "gdn_norm_w": gain((L, GDN_DV)),
        "ssd_conv_w": nrm((L, CONV_K, SSD_XBC), CONV_K ** -0.5),
        "ssd_conv_b": nrm((L, SSD_XBC), 0.02),
        "ssd_a_log": a_log((L, SSD_HEADS)),
        "ssd_dt_bias": dt_bias((L, SSD_HEADS)),
        "ssd_d": gain((L, SSD_HEADS)),
        "ssd_norm_w": gain((L, SSD_INNER)),
        "w_proj_gdn": nrm((L, GDN_HEADS * GDN_DV, D_MODEL), (GDN_HEADS * GDN_DV) ** -0.5),
        "w_proj_ssd": nrm((L, SSD_INNER, D_MODEL), SSD_INNER ** -0.5),
        "w_proj_ret": nrm((L, RET_HEADS * RET_DV, D_MODEL), (RET_HEADS * RET_DV) ** -0.5),
        "mix_w_out": nrm((L, D_MODEL, D_MODEL), D_MODEL ** -0.5 * DN_BETA),
        "ln_mix_g": gain((L, D_MODEL)),
        "ln_mix_b": nrm((L, D_MODEL), 0.02),
        "xa_w_q": nrm((L, D_MODEL, D_MODEL), D_MODEL ** -0.5),
        "xa_w_k": nrm((L, D_MODEL, D_MODEL), D_MODEL ** -0.5),
        "xa_w_v": nrm((L, D_MODEL, D_MODEL), D_MODEL ** -0.5 * DN_BETA),
        "xa_w_o": nrm((L, D_MODEL, D_MODEL), D_MODEL ** -0.5 * DN_BETA),
        "ln_xa_g": gain((L, D_MODEL)),
        "ln_xa_b": nrm((L, D_MODEL), 0.02),
        "router_w": nrm((D_MODEL, N_EXPERTS), D_MODEL ** -0.5),
        "router_b": nrm((N_EXPERTS,), 0.01),
        "moe_w_gate": nrm((L, N_EXPERTS, D_MODEL, D_EXPERT), D_MODEL ** -0.5),
        "moe_w_up": nrm((L, N_EXPERTS, D_MODEL, D_EXPERT), D_MODEL ** -0.5),
        "moe_w_down": nrm((L, N_EXPERTS, D_EXPERT, D_MODEL), D_EXPERT ** -0.5 * DN_BETA),
        "ln_moe_g": gain((L, D_MODEL)),
        "ln_moe_b": nrm((L, D_MODEL), 0.02),
    }


def reference(x, mem, mix_w_in, gdn_conv_w, gdn_a_log, gdn_dt_bias, gdn_norm_w,
              ssd_conv_w, ssd_conv_b, ssd_a_log, ssd_dt_bias, ssd_d, ssd_norm_w,
              w_proj_gdn, w_proj_ssd, w_proj_ret, mix_w_out, ln_mix_g, ln_mix_b,
              xa_w_q, xa_w_k, xa_w_v, xa_w_o, ln_xa_g, ln_xa_b,
              router_w, router_b, moe_w_gate, moe_w_up, moe_w_down, ln_moe_g, ln_moe_b):
    seq = x.shape[1]
    pos = jnp.arange(seq, dtype=jnp.float32)
    inv_freq = ROPE_BASE ** (-jnp.arange(0, RET_DK, 2, dtype=jnp.float32) / RET_DK)
    ang = pos[:, None] * inv_freq[None, :]
    cos, sin = jnp.cos(ang), jnp.sin(ang)
    for l in range(DEPTH):
        mix = hybrid_mixer(x, mix_w_in[l], gdn_conv_w[l], gdn_a_log[l], gdn_dt_bias[l], gdn_norm_w[l],
                           ssd_conv_w[l], ssd_conv_b[l], ssd_a_log[l], ssd_dt_bias[l], ssd_d[l],
                           ssd_norm_w[l], w_proj_gdn[l], w_proj_ssd[l], w_proj_ret[l], mix_w_out[l],
                           cos, sin)
        x = layer_norm(DN_ALPHA * x + mix, ln_mix_g[l], ln_mix_b[l])
        xa = memory_cross_attention(x, mem, xa_w_q[l], xa_w_k[l], xa_w_v[l], xa_w_o[l])
        x = layer_norm(DN_ALPHA * x + xa, ln_xa_g[l], ln_xa_b[l])
        ff = moe_ffn(x, router_w, router_b, moe_w_gate[l], moe_w_up[l], moe_w_down[l])
        x = layer_norm(DN_ALPHA * x + ff, ln_moe_g[l], ln_moe_b[l])
    return x
```

```python
import functools
import math

import jax
import jax.numpy as jnp
from jax import lax
from jax.experimental import pallas as pl
from jax.experimental.pallas import tpu as pltpu

F32 = jnp.float32
BF16 = jnp.bfloat16

D_MODEL = 1024
DEPTH = 4
CHUNK = 64
CONV_K = 4
GDN_HEADS = 8
GDN_DK = 128
GDN_DV = 128
SSD_INNER = D_MODEL
SSD_HEADDIM = 64
SSD_HEADS = SSD_INNER // SSD_HEADDIM
SSD_GROUPS = 4
SSD_HPG = SSD_HEADS // SSD_GROUPS
SSD_STATE = 128
RET_HEADS = 4
RET_DK = 128
RET_DV = 256
ROPE_BASE = 10000.0
N_BRANCH = 3
MEM_LEN = 256
XA_HEADS = 4
XA_DH = D_MODEL // XA_HEADS
N_EXPERTS = 16
N_EXPERT_GROUPS = 4
EXPERTS_PER_GROUP = N_EXPERTS // N_EXPERT_GROUPS
TOP_K = 2
D_EXPERT = 512
DN_ALPHA = (2 * DEPTH) ** 0.25
LN_EPS = 1e-5
RMS_EPS = 1e-6

GDN_QKV = 2 * GDN_HEADS * GDN_DK + GDN_HEADS * GDN_DV
SSD_XBC = SSD_INNER + 2 * SSD_GROUPS * SSD_STATE

LANE = 128
COL_W = 3072
COL_GATE = 0
COL_GDN = 1
COL_RET = 2
COL_SSD = 3
COL_GDN_Z = 12
COL_GDN_AB = 104
COL_SSD_DT = 105
PROJ_N = 13824
HALO = 8

MOE_BLK = 256
VMEM_LIMIT = 48 * 1024 * 1024


def _mm(a, b):
    return jnp.dot(a.astype(BF16), b.astype(BF16), preferred_element_type=F32)


def _mm_nt(a, b):
    return lax.dot_general(a.astype(BF16), b.astype(BF16), (((1,), (1,)), ((), ())),
                           preferred_element_type=F32)


def _mm_tn(a, b):
    return lax.dot_general(a.astype(BF16), b.astype(BF16), (((0,), (0,)), ((), ())),
                           preferred_element_type=F32)


def _mm_f32(a, b):
    return jnp.dot(a, b, preferred_element_type=F32, precision=lax.Precision.HIGHEST)


def _silu(x):
    return x * jax.nn.sigmoid(x)


def _softplus(x):
    return jnp.maximum(x, 0.0) + jnp.log1p(jnp.exp(-jnp.abs(x)))


def _layer_norm(v, g, b):
    mu = jnp.mean(v, -1, keepdims=True)
    var = jnp.mean(jnp.square(v - mu), -1, keepdims=True)
    return (v - mu) * lax.rsqrt(var + LN_EPS) * g + b


def _params(sem):
    return pltpu.CompilerParams(dimension_semantics=sem, vmem_limit_bytes=VMEM_LIMIT)


def _matmul_kernel(x_ref, w_ref, o_ref):
    o_ref[...] = _mm(x_ref[...], w_ref[...]).astype(o_ref.dtype)


def _matmul(x, w, out_dtype, tm, tn):
    m, k = x.shape
    n = w.shape[1]
    tm, tn = min(tm, m), min(tn, n)
    return pl.pallas_call(
        _matmul_kernel,
        grid=(m // tm, n // tn),
        in_specs=[pl.BlockSpec((tm, k), lambda i, j: (i, 0)),
                  pl.BlockSpec((k, tn), lambda i, j: (0, j))],
        out_specs=pl.BlockSpec((tm, tn), lambda i, j: (i, j)),
        out_shape=jax.ShapeDtypeStruct((m, n), out_dtype),
        compiler_params=_params(("parallel", "arbitrary")),
        name="matmul",
    )(x, w)


def _causal_conv(ext_ref, blk, w, first):
    L = blk.shape[0]

    @pl.when(first)
    def _():
        ext_ref[0:HALO, :] = jnp.zeros((HALO, ext_ref.shape[1]), F32)

    ext_ref[HALO:HALO + L, :] = blk
    acc = blk * w[CONV_K - 1:CONV_K, :]
    for j in range(CONV_K - 1):
        off = HALO - (CONV_K - 1) + j
        acc = acc + ext_ref[off:off + L, :] * w[j:j + 1, :]
    ext_ref[0:HALO, :] = blk[L - HALO:L, :]
    return acc


def _tri_masks(L):
    row = lax.broadcasted_iota(jnp.int32, (L, L), 0)
    col = lax.broadcasted_iota(jnp.int32, (L, L), 1)
    return row >= col, row > col


def _upper_ones(L):
    row = lax.broadcasted_iota(jnp.int32, (L, L), 0)
    col = lax.broadcasted_iota(jnp.int32, (L, L), 1)
    return (row <= col).astype(F32)


def _decay_matrix(cum_c, cum_r, causal):
    return jnp.exp(jnp.where(causal, cum_c - cum_r, -jnp.inf))


def _gdn_kernel(x_ref, qkv_ref, z_ref, ab_ref, wabt_ref, convw_ref, pcol_ref, prow_ref, normw_ref,
                o_ref, ext_ref, state_ref):
    L = qkv_ref.shape[0]
    first = pl.program_id(1) == 0

    @pl.when(first)
    def _():
        state_ref[...] = jnp.zeros(state_ref.shape, F32)

    qkv = _silu(_causal_conv(ext_ref, qkv_ref[...], convw_ref[...], first))
    causal, strict = _tri_masks(L)
    tri = causal.astype(F32)
    eye = (causal & ~strict).astype(F32)

    ab = ab_ref[...]
    la_col = -jnp.exp(pcol_ref[0:1, :]) * _softplus(ab + pcol_ref[1:2, :])
    cum_col = _mm_f32(tri, la_col)
    beta_col = jax.nn.sigmoid(ab)
    ab_t = _mm_nt(wabt_ref[...], x_ref[...])
    la_row = -jnp.exp(prow_ref[:, 0:1]) * _softplus(ab_t + prow_ref[:, 1:2])
    cum_row = _mm_f32(la_row, _upper_ones(L))

    for h in range(GDN_HEADS):
        q = qkv[:, h * GDN_DK:(h + 1) * GDN_DK]
        k = qkv[:, GDN_HEADS * GDN_DK + h * GDN_DK:GDN_HEADS * GDN_DK + (h + 1) * GDN_DK]
        v = qkv[:, 2 * GDN_HEADS * GDN_DK + h * GDN_DV:2 * GDN_HEADS * GDN_DK + (h + 1) * GDN_DV]
        q = q * lax.rsqrt(jnp.sum(jnp.square(q), -1, keepdims=True) + RMS_EPS) * GDN_DK ** -0.5
        k = k * lax.rsqrt(jnp.sum(jnp.square(k), -1, keepdims=True) + RMS_EPS)
        cum_c = cum_col[:, h:h + 1]
        cum_r = cum_row[h:h + 1, :]
        beta = beta_col[:, GDN_HEADS + h:GDN_HEADS + h + 1]
        decay = _decay_matrix(cum_c, cum_r, causal)
        kb = k * beta
        vb = v * beta
        a_strict = jnp.where(strict, _mm_nt(kb, k) * decay, 0.0)
        n_pow = -a_strict
        t_mat = eye + n_pow
        for _ in range(int(math.log2(L)) - 1):
            n_pow = _mm_f32(n_pow, n_pow)
            t_mat = t_mat + _mm_f32(t_mat, n_pow)
        e_cum = jnp.exp(cum_c)
        u = _mm(t_mat, vb)
        w = _mm(t_mat, kb * e_cum)
        attn = jnp.where(causal, _mm_nt(q, k) * decay, 0.0)
        last = cum_c[L - 1:L, :]
        q_dec = q * e_cum
        k_dec = k * jnp.exp(last - cum_c)
        state = state_ref[h]
        v_new = u - _mm(w, state)
        o = _mm(q_dec, state) + _mm(attn, v_new)
        state_ref[h] = state * jnp.exp(last) + _mm_tn(k_dec, v_new)
        o = o * lax.rsqrt(jnp.mean(jnp.square(o), -1, keepdims=True) + RMS_EPS) * normw_ref[...]
        o = o * _silu(z_ref[:, h * GDN_DV:(h + 1) * GDN_DV])
        o_ref[:, h * GDN_DV:(h + 1) * GDN_DV] = o.astype(o_ref.dtype)


def _gdn(x3, h3, wabt, convw, pcol, prow, normw):
    bsz, seq, _ = x3.shape
    L = CHUNK
    const = lambda shape: pl.BlockSpec(shape, lambda b, c: (0,) * len(shape))
    return pl.pallas_call(
        _gdn_kernel,
        grid=(bsz, seq // L),
        in_specs=[pl.BlockSpec((None, L, D_MODEL), lambda b, c: (b, c, 0)),
                  pl.BlockSpec((None, L, COL_W), lambda b, c: (b, c, COL_GDN)),
                  pl.BlockSpec((None, L, D_MODEL), lambda b, c: (b, c, COL_GDN_Z)),
                  pl.BlockSpec((None, L, LANE), lambda b, c: (b, c, COL_GDN_AB)),
                  const(wabt.shape), const(convw.shape), const(pcol.shape), const(prow.shape),
                  const(normw.shape)],
        out_specs=pl.BlockSpec((None, L, D_MODEL), lambda b, c: (b, c, 0)),
        out_shape=jax.ShapeDtypeStruct((bsz, seq, GDN_HEADS * GDN_DV), BF16),
        scratch_shapes=[pltpu.VMEM((HALO + L, GDN_QKV), F32),
                        pltpu.VMEM((GDN_HEADS, GDN_DK, GDN_DV), F32)],
        compiler_params=_params(("parallel", "arbitrary")),
        name="gdn",
    )(x3, h3, h3, h3, wabt, convw, pcol, prow, normw)


def _ssd_kernel(x_ref, xbcz_ref, dt_ref, wdtt_ref, convw_ref, convb_ref, pcol_ref, prow_ref, dskip_ref,
                normw_ref, o_ref, ext_ref, state_ref):
    L = xbcz_ref.shape[0]
    first = pl.program_id(1) == 0

    @pl.when(first)
    def _():
        state_ref[...] = jnp.zeros(state_ref.shape, F32)

    xbc = _silu(_causal_conv(ext_ref, xbcz_ref[:, 0:SSD_XBC], convw_ref[...], first) + convb_ref[...])
    causal, _ = _tri_masks(L)
    tri = causal.astype(F32)

    dt_col = _softplus(dt_ref[...] + pcol_ref[1:2, :])
    cum_col = _mm_f32(tri, -jnp.exp(pcol_ref[0:1, :]) * dt_col)
    dt_t = _mm_nt(wdtt_ref[...], x_ref[...])
    la_row = -jnp.exp(prow_ref[:, 0:1]) * _softplus(dt_t + prow_ref[:, 1:2])
    cum_row = _mm_f32(la_row, _upper_ones(L))

    gw = SSD_HPG * SSD_HEADDIM
    for g in range(SSD_GROUPS):
        bm = xbc[:, SSD_INNER + g * SSD_STATE:SSD_INNER + (g + 1) * SSD_STATE]
        cm = xbc[:, SSD_INNER + SSD_GROUPS * SSD_STATE + g * SSD_STATE:
                 SSD_INNER + SSD_GROUPS * SSD_STATE + (g + 1) * SSD_STATE]
        scores = _mm_nt(cm, bm)
        ys = []
        for r in range(SSD_HPG):
            h = g * SSD_HPG + r
            xs = xbc[:, h * SSD_HEADDIM:(h + 1) * SSD_HEADDIM]
            cum_c = cum_col[:, h:h + 1]
            decay = _decay_matrix(cum_c, cum_row[h:h + 1, :], causal)
            v = xs * dt_col[:, h:h + 1]
            last = cum_c[L - 1:L, :]
            state = state_ref[h]
            y = _mm(scores * decay, v) + _mm(cm, state) * jnp.exp(cum_c)
            state_ref[h] = state * jnp.exp(last) + _mm_tn(bm, v * jnp.exp(last - cum_c))
            ys.append(y + dskip_ref[:, h * SSD_HEADDIM:(h + 1) * SSD_HEADDIM] * xs)
        y = jnp.concatenate(ys, axis=-1) * _silu(xbcz_ref[:, SSD_XBC + g * gw:SSD_XBC + (g + 1) * gw])
        y = y * lax.rsqrt(jnp.mean(jnp.square(y), -1, keepdims=True) + RMS_EPS)
        o_ref[:, g * gw:(g + 1) * gw] = (y * normw_ref[:, g * gw:(g + 1) * gw]).astype(o_ref.dtype)


def _ssd(x3, h3, wdtt, convw, convb, pcol, prow, dskip, normw):
    bsz, seq, _ = x3.shape
    L = CHUNK
    const = lambda shape: pl.BlockSpec(shape, lambda b, c: (0,) * len(shape))
    return pl.pallas_call(
        _ssd_kernel,
        grid=(bsz, seq // L),
        in_specs=[pl.BlockSpec((None, L, D_MODEL), lambda b, c: (b, c, 0)),
                  pl.BlockSpec((None, L, COL_W), lambda b, c: (b, c, COL_SSD)),
                  pl.BlockSpec((None, L, LANE), lambda b, c: (b, c, COL_SSD_DT)),
                  const(wdtt.shape), const(convw.shape), const(convb.shape), const(pcol.shape),
                  const(prow.shape), const(dskip.shape), const(normw.shape)],
        out_specs=pl.BlockSpec((None, L, D_MODEL), lambda b, c: (b, c, 0)),
        out_shape=jax.ShapeDtypeStruct((bsz, seq, SSD_INNER), BF16),
        scratch_shapes=[pltpu.VMEM((HALO + L, SSD_XBC), F32),
                        pltpu.VMEM((SSD_HEADS, SSD_STATE, SSD_HEADDIM), F32)],
        compiler_params=_params(("parallel", "arbitrary")),
        name="ssd",
    )(x3, h3, h3, wdtt, convw, convb, pcol, prow, dskip, normw)


def _ret_kernel(qkvg_ref, cos_ref, sin_ref, o_ref, state_ref):
    L = qkvg_ref.shape[0]

    @pl.when(pl.program_id(1) == 0)
    def _():
        state_ref[...] = jnp.zeros(state_ref.shape, F32)

    causal, _ = _tri_masks(L)
    row = lax.broadcasted_iota(jnp.int32, (L, L), 0)
    col = lax.broadcasted_iota(jnp.int32, (L, L), 1)
    dist = (row - col).astype(F32)
    pos = lax.broadcasted_iota(jnp.int32, (L, 1), 0).astype(F32)
    cos2 = cos_ref[...]
    sin2 = sin_ref[...]
    k_off = RET_HEADS * RET_DK
    v_off = 2 * RET_HEADS * RET_DK
    g_off = v_off + RET_HEADS * RET_DV
    for h in range(RET_HEADS):
        log_gamma = math.log1p(-2.0 ** (-5.0 - h))
        q = qkvg_ref[:, h * RET_DK:(h + 1) * RET_DK]
        k = qkvg_ref[:, k_off + h * RET_DK:k_off + (h + 1) * RET_DK]
        v = qkvg_ref[:, v_off + h * RET_DV:v_off + (h + 1) * RET_DV]
        q = q * cos2 + pltpu.roll(q, RET_DK // 2, 1) * sin2
        k = (k * cos2 + pltpu.roll(k, RET_DK // 2, 1) * sin2) * RET_DK ** -0.5
        decay = jnp.exp(jnp.where(causal, dist * log_gamma, -jnp.inf))
        state = state_ref[h]
        o = _mm(_mm_nt(q, k) * decay, v) + _mm(q, state) * jnp.exp((pos + 1.0) * log_gamma)
        state_ref[h] = state * math.exp(L * log_gamma) + _mm_tn(k, v * jnp.exp((L - 1.0 - pos) * log_gamma))
        mu = jnp.mean(o, -1, keepdims=True)
        var = jnp.mean(jnp.square(o - mu), -1, keepdims=True)
        o = (o - mu) * lax.rsqrt(var + LN_EPS)
        o = _silu(qkvg_ref[:, g_off + h * RET_DV:g_off + (h + 1) * RET_DV]) * o
        o_ref[:, h * RET_DV:(h + 1) * RET_DV] = o.astype(o_ref.dtype)


def _ret(h3, cos2, sin2):
    bsz, seq, _ = h3.shape
    L = CHUNK
    return pl.pallas_call(
        _ret_kernel,
        grid=(bsz, seq // L),
        in_specs=[pl.BlockSpec((None, L, COL_W), lambda b, c: (b, c, COL_RET)),
                  pl.BlockSpec((L, RET_DK), lambda b, c: (c, 0)),
                  pl.BlockSpec((L, RET_DK), lambda b, c: (c, 0))],
        out_specs=pl.BlockSpec((None, L, D_MODEL), lambda b, c: (b, c, 0)),
        out_shape=jax.ShapeDtypeStruct((bsz, seq, RET_HEADS * RET_DV), BF16),
        scratch_shapes=[pltpu.VMEM((RET_HEADS, RET_DK, RET_DV), F32)],
        compiler_params=_params(("parallel", "arbitrary")),
        name="ret",
    )(h3, cos2, sin2)


def _merge_kernel(x_ref, gate_ref, yg_ref, ys_ref, yr_ref, wg_ref, ws_ref, wr_ref, wo_ref, g_ref, b_ref,
                  o_ref):
    d = D_MODEL
    merged = (jax.nn.sigmoid(gate_ref[:, 0:d]) * _mm(yg_ref[...], wg_ref[...])
              + jax.nn.sigmoid(gate_ref[:, d:2 * d]) * _mm(ys_ref[...], ws_ref[...])
              + jax.nn.sigmoid(gate_ref[:, 2 * d:3 * d]) * _mm(yr_ref[...], wr_ref[...]))
    mix = _mm(merged, wo_ref[...])
    o_ref[...] = _layer_norm(DN_ALPHA * x_ref[...] + mix, g_ref[...], b_ref[...])


def _merge(x2, h2, yg, ys, yr, wg, ws, wr, wo, g, b, tm=512):
    t = x2.shape[0]
    tm = min(tm, t)
    row = lambda w: pl.BlockSpec((tm, w), lambda i: (i, 0))
    const = lambda shape: pl.BlockSpec(shape, lambda i: (0,) * len(shape))
    return pl.pallas_call(
        _merge_kernel,
        grid=(t // tm,),
        in_specs=[row(D_MODEL), pl.BlockSpec((tm, COL_W), lambda i: (i, COL_GATE)),
                  row(D_MODEL), row(D_MODEL), row(D_MODEL),
                  const(wg.shape), const(ws.shape), const(wr.shape), const(wo.shape),
                  const(g.shape), const(b.shape)],
        out_specs=row(D_MODEL),
        out_shape=jax.ShapeDtypeStruct((t, D_MODEL), F32),
        compiler_params=_params(("parallel",)),
        name="merge",
    )(x2, h2, yg, ys, yr, wg, ws, wr, wo, g, b)


def _xattn_kernel(x_ref, kv_ref, wq_ref, wo_ref, g_ref, b_ref, o_ref):
    x = x_ref[...]
    q = _mm(x, wq_ref[...])
    outs = []
    for h in range(XA_HEADS):
        k = kv_ref[:, h * XA_DH:(h + 1) * XA_DH]
        v = kv_ref[:, D_MODEL + h * XA_DH:D_MODEL + (h + 1) * XA_DH]
        s = _mm_nt(q[:, h * XA_DH:(h + 1) * XA_DH], k) * XA_DH ** -0.5
        e = jnp.exp(s - jnp.max(s, -1, keepdims=True))
        outs.append(_mm(e / jnp.sum(e, -1, keepdims=True), v))
    xa = _mm(jnp.concatenate(outs, axis=-1), wo_ref[...])
    o_ref[...] = _layer_norm(DN_ALPHA * x + xa, g_ref[...], b_ref[...])


def _xattn(x3, kv3, wq, wo, g, b, tm=512):
    bsz, seq, _ = x3.shape
    tm = min(tm, seq)
    const = lambda shape: pl.BlockSpec(shape, lambda bb, i: (0,) * len(shape))
    return pl.pallas_call(
        _xattn_kernel,
        grid=(bsz, seq // tm),
        in_specs=[pl.BlockSpec((None, tm, D_MODEL), lambda bb, i: (bb, i, 0)),
                  pl.BlockSpec((None, MEM_LEN, 2 * D_MODEL), lambda bb, i: (bb, 0, 0)),
                  const(wq.shape), const(wo.shape), const(g.shape), const(b.shape)],
        out_specs=pl.BlockSpec((None, tm, D_MODEL), lambda bb, i: (bb, i, 0)),
        out_shape=jax.ShapeDtypeStruct((bsz, seq, D_MODEL), F32),
        compiler_params=_params(("parallel", "parallel")),
        name="xattn",
    )(x3, kv3, wq, wo, g, b)


def _top2_sum(a, b, c, d):
    return jnp.maximum(jnp.maximum(a, b) + jnp.maximum(c, d), jnp.maximum(a + b, c + d))


def _router_kernel(x_ref, wrt_ref, br_ref, e_ref, w_ref, rank_ref, cnt_ref, carry_ref):
    tm = x_ref.shape[0]
    epg = EXPERTS_PER_GROUP

    @pl.when(pl.program_id(0) == 0)
    def _():
        carry_ref[...] = jnp.zeros(carry_ref.shape, F32)

    scores = jax.nn.sigmoid(_mm_nt(wrt_ref[...], x_ref[...]))
    sel = scores + br_ref[...]
    rows = [sel[i:i + 1, :] for i in range(N_EXPERTS)]
    srow = [scores[i:i + 1, :] for i in range(N_EXPERTS)]
    gscore = [_top2_sum(*rows[g * epg:(g + 1) * epg]) for g in range(N_EXPERT_GROUPS)]
    best, gidx = gscore[0], jnp.zeros((1, tm), jnp.int32)
    for g in range(1, N_EXPERT_GROUPS):
        better = gscore[g] > best
        best = jnp.where(better, gscore[g], best)
        gidx = jnp.where(better, g, gidx)
    pick = lambda vals, j: functools.reduce(
        lambda acc, g: jnp.where(gidx == g, vals[g * epg + j], acc), range(1, N_EXPERT_GROUPS), vals[j])
    ing = [pick(rows, j) for j in range(epg)]
    ins = [pick(srow, j) for j in range(epg)]
    v0, l0, s0 = ing[0], jnp.zeros((1, tm), jnp.int32), ins[0]
    for j in range(1, epg):
        better = ing[j] > v0
        v0 = jnp.where(better, ing[j], v0)
        l0 = jnp.where(better, j, l0)
        s0 = jnp.where(better, ins[j], s0)
    v1 = jnp.full((1, tm), -jnp.inf, F32)
    l1 = jnp.zeros((1, tm), jnp.int32)
    s1 = jnp.zeros((1, tm), F32)
    for j in range(epg):
        better = (ing[j] > v1) & (l0 != j)
        v1 = jnp.where(better, ing[j], v1)
        l1 = jnp.where(better, j, l1)
        s1 = jnp.where(better, ins[j], s1)
    e0 = gidx * epg + l0
    e1 = gidx * epg + l1
    tot = s0 + s1
    e_ref[0:1, :] = e0
    e_ref[1:2, :] = e1
    w_ref[0:1, :] = s0 / tot
    w_ref[1:2, :] = s1 / tot
    eid = lax.broadcasted_iota(jnp.int32, (N_EXPERTS, tm), 0)
    oh0 = (eid == e0).astype(F32)
    oh1 = (eid == e1).astype(F32)
    ti = lax.broadcasted_iota(jnp.int32, (tm, tm), 0)
    tj = lax.broadcasted_iota(jnp.int32, (tm, tm), 1)
    before = _mm(oh0 + oh1, (ti < tj).astype(F32)) + carry_ref[...]
    rank_ref[0:1, :] = jnp.sum(oh0 * before, 0, keepdims=True).astype(jnp.int32)
    rank_ref[1:2, :] = jnp.sum(oh1 * before, 0, keepdims=True).astype(jnp.int32)
    carry_ref[...] = carry_ref[...] + jnp.sum(oh0 + oh1, 1, keepdims=True)
    cnt_ref[...] = carry_ref[...].astype(jnp.int32)


def _router(x2, wrt, br, tm=512):
    t = x2.shape[0]
    tm = min(tm, t)
    const = lambda shape: pl.BlockSpec(shape, lambda i: (0,) * len(shape))
    tok = pl.BlockSpec((TOP_K, tm), lambda i: (0, i))
    return pl.pallas_call(
        _router_kernel,
        grid=(t // tm,),
        in_specs=[pl.BlockSpec((tm, D_MODEL), lambda i: (i, 0)), const(wrt.shape), const(br.shape)],
        out_specs=[tok, tok, tok, const((N_EXPERTS, 1))],
        out_shape=[jax.ShapeDtypeStruct((TOP_K, t), jnp.int32), jax.ShapeDtypeStruct((TOP_K, t), F32),
                   jax.ShapeDtypeStruct((TOP_K, t), jnp.int32),
                   jax.ShapeDtypeStruct((N_EXPERTS, 1), jnp.int32)],
        scratch_shapes=[pltpu.VMEM((N_EXPERTS, 1), F32)],
        compiler_params=_params(("arbitrary",)),
        name="router",
    )(x2, wrt, br)


def _ffn_kernel(be_ref, nused_ref, x_ref, wgu_ref, wd_ref, o_ref):
    i = pl.program_id(0)

    @pl.when(i < nused_ref[0])
    def _():
        gu = _mm(x_ref[...], wgu_ref[...])
        hid = _silu(gu[:, 0:D_EXPERT]) * gu[:, D_EXPERT:2 * D_EXPERT]
        o_ref[...] = _mm(hid, wd_ref[...])

    @pl.when(i >= nused_ref[0])
    def _():
        o_ref[...] = jnp.zeros(o_ref.shape, o_ref.dtype)


def _ffn(block_expert, n_used, xb, wgu, wd):
    npad = xb.shape[0]
    return pl.pallas_call(
        _ffn_kernel,
        grid_spec=pltpu.PrefetchScalarGridSpec(
            num_scalar_prefetch=2,
            grid=(npad // MOE_BLK,),
            in_specs=[pl.BlockSpec((MOE_BLK, D_MODEL), lambda i, be, nu: (i, 0)),
                      pl.BlockSpec((None, D_MODEL, 2 * D_EXPERT), lambda i, be, nu: (be[i], 0, 0)),
                      pl.BlockSpec((None, D_EXPERT, D_MODEL), lambda i, be, nu: (be[i], 0, 0))],
            out_specs=pl.BlockSpec((MOE_BLK, D_MODEL), lambda i, be, nu: (i, 0))),
        out_shape=jax.ShapeDtypeStruct((npad, D_MODEL), F32),
        compiler_params=_params(("arbitrary",)),
        name="ffn",
    )(block_expert, n_used, xb, wgu, wd)


def _combine_kernel(x_ref, y0_ref, y1_ref, w_ref, g_ref, b_ref, o_ref):
    ff = y0_ref[...] * w_ref[:, 0:1] + y1_ref[...] * w_ref[:, 1:2]
    o_ref[...] = _layer_norm(DN_ALPHA * x_ref[...] + ff, g_ref[...], b_ref[...])


def _combine(x2, y0, y1, w, g, b, tm=512):
    t = x2.shape[0]
    tm = min(tm, t)
    row = lambda w_: pl.BlockSpec((tm, w_), lambda i: (i, 0))
    const = lambda shape: pl.BlockSpec(shape, lambda i: (0,) * len(shape))
    return pl.pallas_call(
        _combine_kernel,
        grid=(t // tm,),
        in_specs=[row(D_MODEL), row(D_MODEL), row(D_MODEL), row(TOP_K), const(g.shape), const(b.shape)],
        out_specs=row(D_MODEL),
        out_shape=jax.ShapeDtypeStruct((t, D_MODEL), F32),
        compiler_params=_params(("parallel",)),
        name="combine",
    )(x2, y0, y1, w, g, b)


def _moe(x2, wrt, br, wgu, wd, ln_g, ln_b):
    t = x2.shape[0]
    expert, gate, rank, counts = _router(x2, wrt, br)
    counts = counts[:, 0]
    padded = (counts + MOE_BLK - 1) // MOE_BLK * MOE_BLK
    pend = jnp.cumsum(padded)
    pstart = pend - padded
    n_blocks = (t * TOP_K + N_EXPERTS * (MOE_BLK - 1) + MOE_BLK - 1) // MOE_BLK
    npad = n_blocks * MOE_BLK
    dest = pstart[expert] + rank
    tok = jnp.broadcast_to(jnp.arange(t, dtype=jnp.int32)[None, :], (TOP_K, t))
    buf_tok = jnp.zeros((npad,), jnp.int32).at[dest.reshape(-1)].set(tok.reshape(-1))
    block_expert = jnp.minimum(
        jnp.searchsorted(pend, jnp.arange(n_blocks, dtype=jnp.int32) * MOE_BLK, side='right'),
        N_EXPERTS - 1).astype(jnp.int32)
    n_used = (pend[-1:] // MOE_BLK).astype(jnp.int32)
    xb = x2.astype(BF16)[buf_tok]
    yb = _ffn(block_expert, n_used, xb, wgu, wd)
    return _combine(x2, yb[dest[0]], yb[dest[1]], gate.T, ln_g, ln_b)


def _prep_weights(mix_w_in, gdn_a_log, gdn_dt_bias, ssd_a_log, ssd_dt_bias, ssd_d, moe_w_gate, moe_w_up,
                  xa_w_k, xa_w_v, router_w, router_b):
    nl = mix_w_in.shape[0]
    off = [0]
    for s in (GDN_QKV, GDN_HEADS * GDN_DV, GDN_HEADS, GDN_HEADS, SSD_XBC, SSD_INNER, SSD_HEADS,
              RET_HEADS * RET_DK, RET_HEADS * RET_DK, RET_HEADS * RET_DV, RET_HEADS * RET_DV,
              N_BRANCH * D_MODEL):
        off.append(off[-1] + s)
    sl = lambda i, j=None: mix_w_in[:, :, off[i]:off[(i if j is None else j) + 1]]
    zeros = lambda n: jnp.zeros((nl, D_MODEL, n), mix_w_in.dtype)
    w_all = jnp.concatenate([
        sl(11),
        sl(0),
        sl(7, 10),
        sl(4, 5),
        sl(1),
        sl(2, 3), zeros(LANE - 2 * GDN_HEADS),
        sl(6), zeros(LANE - SSD_HEADS),
        zeros(PROJ_N - (COL_SSD_DT + 1) * LANE)], axis=-1).astype(BF16)
    wabt = jnp.swapaxes(sl(2, 3), 1, 2).astype(BF16)
    wdtt = jnp.swapaxes(sl(6), 1, 2).astype(BF16)

    def col_params(a_log, dt_bias, lane0):
        n = a_log.shape[1]
        p = jnp.zeros((nl, 2, LANE), F32)
        return p.at[:, 0, lane0:lane0 + n].set(a_log).at[:, 1, lane0:lane0 + n].set(dt_bias)

    def row_params(a_log, dt_bias):
        n = a_log.shape[1]
        p = jnp.zeros((nl, 16, 2), F32)
        return p.at[:, 0:n, 0].set(a_log).at[:, 0:n, 1].set(dt_bias)

    return dict(
        w_all=w_all, wabt=wabt, wdtt=wdtt,
        gdn_pcol=col_params(gdn_a_log, gdn_dt_bias, 0), gdn_prow=row_params(gdn_a_log, gdn_dt_bias),
        ssd_pcol=col_params(ssd_a_log, ssd_dt_bias, 0), ssd_prow=row_params(ssd_a_log, ssd_dt_bias),
        ssd_dskip=jnp.repeat(ssd_d, SSD_HEADDIM, axis=-1)[:, None, :],
        wgu=jnp.concatenate([moe_w_gate, moe_w_up], axis=-1).astype(BF16),
        wkv=jnp.concatenate([xa_w_k, xa_w_v], axis=-1).astype(BF16),
        wrt=router_w.T.astype(BF16), br=router_b[:, None],
    )


def kernel(x, mem, mix_w_in, gdn_conv_w, gdn_a_log, gdn_dt_bias, gdn_norm_w, ssd_conv_w, ssd_conv_b, ssd_a_log, ssd_dt_bias, ssd_d, ssd_norm_w, w_proj_gdn, w_proj_ssd, w_proj_ret, mix_w_out, ln_mix_g, ln_mix_b, xa_w_q, xa_w_k, xa_w_v, xa_w_o, ln_xa_g, ln_xa_b, router_w, router_b, moe_w_gate, moe_w_up, moe_w_down, ln_moe_g, ln_moe_b):
    bsz, seq, d = x.shape
    t = bsz * seq
    p = _prep_weights(mix_w_in, gdn_a_log, gdn_dt_bias, ssd_a_log, ssd_dt_bias, ssd_d, moe_w_gate,
                      moe_w_up, xa_w_k, xa_w_v, router_w, router_b)
    bf = lambda w: w.astype(BF16)
    wpg, wps, wpr, wout = bf(w_proj_gdn), bf(w_proj_ssd), bf(w_proj_ret), bf(mix_w_out)
    wq, wo, wd = bf(xa_w_q), bf(xa_w_o), bf(moe_w_down)
    vec = lambda v, l: v[l][None, :]

    pos = jnp.arange(seq, dtype=F32)
    inv_freq = ROPE_BASE ** (-jnp.arange(0, RET_DK, 2, dtype=F32) / RET_DK)
    ang = pos[:, None] * inv_freq[None, :]
    cos, sin = jnp.cos(ang), jnp.sin(ang)
    cos2 = jnp.concatenate([cos, cos], axis=-1)
    sin2 = jnp.concatenate([-sin, sin], axis=-1)
    mem2 = mem.reshape(bsz * mem.shape[1], d)

    x2 = x.reshape(t, d)
    for l in range(DEPTH):
        h2 = _matmul(x2, p["w_all"][l], F32, 1024, 1536)
        x3, h3 = x2.reshape(bsz, seq, d), h2.reshape(bsz, seq, PROJ_N)
        yg = _gdn(x3, h3, p["wabt"][l], gdn_conv_w[l], p["gdn_pcol"][l], p["gdn_prow"][l],
                  vec(gdn_norm_w, l))
        ys = _ssd(x3, h3, p["wdtt"][l], ssd_conv_w[l], vec(ssd_conv_b, l), p["ssd_pcol"][l],
                  p["ssd_prow"][l], p["ssd_dskip"][l], vec(ssd_norm_w, l))
        yr = _ret(h3, cos2, sin2)
        x2 = _merge(x2, h2, yg.reshape(t, d), ys.reshape(t, d), yr.reshape(t, d), wpg[l], wps[l], wpr[l],
                    wout[l], vec(ln_mix_g, l), vec(ln_mix_b, l))
        kv = _matmul(mem2, p["wkv"][l], BF16, 512, 1024)
        x2 = _xattn(x2.reshape(bsz, seq, d), kv.reshape(bsz, mem.shape[1], 2 * d), wq[l], wo[l],
                    vec(ln_xa_g, l), vec(ln_xa_b, l)).reshape(t, d)
        x2 = _moe(x2, p["wrt"], p["br"], p["wgu"][l], wd[l], vec(ln_moe_g, l), vec(ln_moe_b, l))
    return x2.reshape(bsz, seq, d)
```

```python
import functools
import math

import jax
import jax.numpy as jnp
from jax import lax
from jax.experimental import pallas as pl
from jax.experimental.pallas import tpu as pltpu

F32 = jnp.float32
BF16 = jnp.bfloat16

D_MODEL = 1024
DEPTH = 4
CHUNK = 64
CONV_K = 4
GDN_HEADS = 8
GDN_DK = 128
GDN_DV = 128
SSD_INNER = D_MODEL
SSD_HEADDIM = 64
SSD_HEADS = SSD_INNER // SSD_HEADDIM
SSD_GROUPS = 4
SSD_HPG = SSD_HEADS // SSD_GROUPS
SSD_STATE = 128
RET_HEADS = 4
RET_DK = 128
RET_DV = 256
ROPE_BASE = 10000.0
N_BRANCH = 3
MEM_LEN = 256
XA_HEADS = 4
XA_DH = D_MODEL // XA_HEADS
N_EXPERTS = 16
N_EXPERT_GROUPS = 4
EXPERTS_PER_GROUP = N_EXPERTS // N_EXPERT_GROUPS
TOP_K = 2
D_EXPERT = 512
DN_ALPHA = (2 * DEPTH) ** 0.25
LN_EPS = 1e-5
RMS_EPS = 1e-6

GDN_QKV = 2 * GDN_HEADS * GDN_DK + GDN_HEADS * GDN_DV
SSD_XBC = SSD_INNER + 2 * SSD_GROUPS * SSD_STATE

LANE = 128
COL_W = 3072
COL_GATE = 0
COL_GDN = 1
COL_RET = 2
COL_SSD = 3
COL_GDN_Z = 12
COL_GDN_AB = 104
COL_SSD_DT = 105
PROJ_N = 13824
PROJ_TM = 1024
PROJ_TN = 1536
PROJ_HALO = 16
PROJ_CONV_TILES = (COL_GDN * COL_W // PROJ_TN, COL_GDN * COL_W // PROJ_TN + 1, COL_SSD * COL_W // PROJ_TN)
PROJ_CONV_PART_TILE = COL_SSD * COL_W // PROJ_TN + 1
PROJ_CONV_PART_COLS = SSD_XBC - PROJ_TN
PROJ_VMEM_LIMIT = 56 * 1024 * 1024
GDN_TILE = 128
SSD_TILE = 128
RET_TILE = 256

MOE_BLK = 256
VMEM_LIMIT = 48 * 1024 * 1024


def _mm(a, b):
    return jnp.dot(a.astype(BF16), b.astype(BF16), preferred_element_type=F32)


def _mm_nt(a, b):
    return lax.dot_general(a.astype(BF16), b.astype(BF16), (((1,), (1,)), ((), ())),
                           preferred_element_type=F32)


def _mm_tn(a, b):
    return lax.dot_general(a.astype(BF16), b.astype(BF16), (((0,), (0,)), ((), ())),
                           preferred_element_type=F32)


def _mm_f32(a, b):
    return jnp.dot(a, b, preferred_element_type=F32, precision=lax.Precision.HIGHEST)


def _silu(x):
    hx = 0.5 * x
    return hx + hx * jnp.tanh(hx)


def _softplus(x):
    return jnp.maximum(x, 0.0) + jnp.log1p(jnp.exp(-jnp.abs(x)))


def _layer_norm(v, g, b):
    mu = jnp.mean(v, -1, keepdims=True)
    var = jnp.mean(jnp.square(v - mu), -1, keepdims=True)
    return (v - mu) * lax.rsqrt(var + LN_EPS) * g + b


def _params(sem):
    return pltpu.CompilerParams(dimension_semantics=sem, vmem_limit_bytes=VMEM_LIMIT)


def _matmul_kernel(x_ref, w_ref, o_ref):
    o_ref[...] = _mm(x_ref[...], w_ref[...]).astype(o_ref.dtype)


def _matmul(x, w, out_dtype, tm, tn):
    m, k = x.shape
    n = w.shape[1]
    tm, tn = min(tm, m), min(tn, n)
    return pl.pallas_call(
        _matmul_kernel,
        grid=(m // tm, n // tn),
        in_specs=[pl.BlockSpec((tm, k), lambda i, j: (i, 0)),
                  pl.BlockSpec((k, tn), lambda i, j: (0, j))],
        out_specs=pl.BlockSpec((tm, tn), lambda i, j: (i, j)),
        out_shape=jax.ShapeDtypeStruct((m, n), out_dtype),
        compiler_params=_params(("parallel", "arbitrary")),
        name="matmul",
    )(x, w)


def _proj_kernel(x_ref, xprev_ref, w_ref, cw_ref, cb_ref, o_ref, xs_ref, res_ref, *, tiles_per_seq):
    i, j = pl.program_id(0), pl.program_id(1)
    tm, tn = o_ref.shape

    @pl.when(j == 0)
    def _():
        xs_ref[PROJ_HALO:PROJ_HALO + tm, :] = x_ref[...].astype(BF16)
        starts_sequence = (i % tiles_per_seq) == 0
        xs_ref[0:PROJ_HALO, :] = jnp.where(starts_sequence, 0.0, xprev_ref[...]).astype(BF16)

    r = jnp.dot(xs_ref[...], w_ref[...], preferred_element_type=F32)

    def conv_silu(ncols):
        res_ref[:, 0:ncols] = r[:, 0:ncols]
        acc = r[PROJ_HALO:, 0:ncols] * cw_ref[CONV_K - 1:CONV_K, 0:ncols] + cb_ref[:, 0:ncols]
        for tap in range(CONV_K - 1):
            off = PROJ_HALO - (CONV_K - 1) + tap
            acc = acc + res_ref[off:off + tm, 0:ncols] * cw_ref[tap:tap + 1, 0:ncols]
        return _silu(acc)

    conv_full = (j == PROJ_CONV_TILES[0]) | (j == PROJ_CONV_TILES[1]) | (j == PROJ_CONV_TILES[2])
    conv_part = j == PROJ_CONV_PART_TILE

    @pl.when(conv_full)
    def _():
        o_ref[...] = conv_silu(tn)

    @pl.when(conv_part)
    def _():
        o_ref[:, 0:PROJ_CONV_PART_COLS] = conv_silu(PROJ_CONV_PART_COLS)
        o_ref[:, PROJ_CONV_PART_COLS:] = r[PROJ_HALO:, PROJ_CONV_PART_COLS:]

    @pl.when(jnp.logical_not(conv_full | conv_part))
    def _():
        o_ref[...] = r[PROJ_HALO:, :]


def _proj(x2, w, cw, cb, seq):
    t, d = x2.shape
    tm = min(PROJ_TM, seq)
    tn = PROJ_TN
    halo_blocks = tm // PROJ_HALO
    return pl.pallas_call(
        functools.partial(_proj_kernel, tiles_per_seq=seq // tm),
        grid=(t // tm, PROJ_N // tn),
        in_specs=[pl.BlockSpec((tm, d), lambda i, j: (i, 0)),
                  pl.BlockSpec((PROJ_HALO, d), lambda i, j: (jnp.maximum(i * halo_blocks - 1, 0), 0)),
                  pl.BlockSpec((d, tn), lambda i, j: (0, j)),
                  pl.BlockSpec((CONV_K, tn), lambda i, j: (0, j)),
                  pl.BlockSpec((1, tn), lambda i, j: (0, j))],
        out_specs=pl.BlockSpec((tm, tn), lambda i, j: (i, j)),
        out_shape=jax.ShapeDtypeStruct((t, PROJ_N), F32),
        scratch_shapes=[pltpu.VMEM((PROJ_HALO + tm, d), BF16),
                        pltpu.VMEM((PROJ_HALO + tm, tn), F32)],
        compiler_params=pltpu.CompilerParams(dimension_semantics=("parallel", "arbitrary"),
                                             vmem_limit_bytes=PROJ_VMEM_LIMIT),
        name="proj",
    )(x2, x2, w, cw, cb)


def _tri_masks(n):
    row = lax.broadcasted_iota(jnp.int32, (n, n), 0)
    col = lax.broadcasted_iota(jnp.int32, (n, n), 1)
    return row >= col, row > col


def _upper_ones(n):
    row = lax.broadcasted_iota(jnp.int32, (n, n), 0)
    col = lax.broadcasted_iota(jnp.int32, (n, n), 1)
    return (row <= col).astype(F32)


def _decay_matrix(cum_c, cum_r, causal):
    return jnp.exp(jnp.where(causal, cum_c - cum_r, -jnp.inf))


def _chunk_rows(c):
    return slice(c * CHUNK, (c + 1) * CHUNK)


def _chunk_cumsums(la_col, la_row, nch):
    tri = _tri_masks(CHUNK)[0].astype(F32)
    upper = _upper_ones(CHUNK)
    cum_col = [_mm_f32(tri, la_col[_chunk_rows(c), :]) for c in range(nch)]
    cum_row = [_mm_f32(la_row[:, _chunk_rows(c)], upper) for c in range(nch)]
    return cum_col, cum_row


def _gdn_kernel(x_ref, qkv_ref, z_ref, ab_ref, wabt_ref, pcol_ref, prow_ref, normw_ref, o_ref, state_ref):
    L = qkv_ref.shape[0]
    C = CHUNK
    nch = L // C
    H = range(GDN_HEADS)

    @pl.when(pl.program_id(1) == 0)
    def _():
        state_ref[...] = jnp.zeros(state_ref.shape, F32)

    causal, strict = _tri_masks(C)
    eye = (causal & ~strict).astype(F32)

    ab = ab_ref[...]
    la_col = -jnp.exp(pcol_ref[0:1, :]) * _softplus(ab + pcol_ref[1:2, :])
    beta_col = jax.nn.sigmoid(ab)
    ab_t = _mm_nt(wabt_ref[...], x_ref[...])
    la_row = -jnp.exp(prow_ref[:, 0:1]) * _softplus(ab_t + prow_ref[:, 1:2])
    cum_col, cum_row = _chunk_cumsums(la_col, la_row, nch)

    P = [(c, h) for c in range(nch) for h in H]
    I = range(len(P))
    k_off, v_off = GDN_HEADS * GDN_DK, 2 * GDN_HEADS * GDN_DK
    qs = [qkv_ref[_chunk_rows(c), h * GDN_DK:(h + 1) * GDN_DK] for c, h in P]
    ks = [qkv_ref[_chunk_rows(c), k_off + h * GDN_DK:k_off + (h + 1) * GDN_DK] for c, h in P]
    vs = [qkv_ref[_chunk_rows(c), v_off + h * GDN_DV:v_off + (h + 1) * GDN_DV] for c, h in P]
    qs = [q * (lax.rsqrt(jnp.sum(jnp.square(q), -1, keepdims=True) + RMS_EPS) * GDN_DK ** -0.5) for q in qs]
    ks = [k * lax.rsqrt(jnp.sum(jnp.square(k), -1, keepdims=True) + RMS_EPS) for k in ks]
    cum_c = [cum_col[c][:, h:h + 1] for c, h in P]
    beta = [beta_col[_chunk_rows(c), GDN_HEADS + h:GDN_HEADS + h + 1] for c, h in P]
    decay = [_decay_matrix(cum_c[i], cum_row[c][h:h + 1, :], causal) for i, (c, h) in enumerate(P)]
    kb = [ks[i] * beta[i] for i in I]
    vb = [vs[i] * beta[i] for i in I]
    kk = [_mm_nt(kb[i], ks[i]) for i in I]
    qk = [_mm_nt(qs[i], ks[i]) for i in I]
    n_pow = [-jnp.where(strict, kk[i] * decay[i], 0.0) for i in I]
    t_mat = [eye + n_pow[i] for i in I]
    for _ in range(int(math.log2(C)) - 1):
        n_pow = [_mm(n_pow[i], n_pow[i]) for i in I]
        upd = [_mm(t_mat[i], n_pow[i]) for i in I]
        t_mat = [t_mat[i] + upd[i] for i in I]
    e_cum = [jnp.exp(cum_c[i]) for i in I]
    u = [_mm(t_mat[i], vb[i]) for i in I]
    w = [_mm(t_mat[i], kb[i] * e_cum[i]) for i in I]
    attn = [jnp.where(causal, qk[i] * decay[i], 0.0) for i in I]
    last = [cum_c[i][C - 1:C, :] for i in I]
    q_dec = [qs[i] * e_cum[i] for i in I]
    k_dec = [ks[i] * jnp.exp(last[i] - cum_c[i]) for i in I]

    states = [state_ref[h] for h in H]
    for c in range(nch):
        ix = [c * GDN_HEADS + h for h in H]
        w_s = [_mm(w[i], states[h]) for h, i in zip(H, ix)]
        q_s = [_mm(q_dec[i], states[h]) for h, i in zip(H, ix)]
        v_new = [u[i] - w_s[h] for h, i in zip(H, ix)]
        a_v = [_mm(attn[i], v_new[h]) for h, i in zip(H, ix)]
        k_v = [_mm_tn(k_dec[i], v_new[h]) for h, i in zip(H, ix)]
        states = [states[h] * jnp.exp(last[i]) + k_v[h] for h, i in zip(H, ix)]
        for h in H:
            o = q_s[h] + a_v[h]
            o = o * lax.rsqrt(jnp.mean(jnp.square(o), -1, keepdims=True) + RMS_EPS) * normw_ref[...]
            o = o * _silu(z_ref[_chunk_rows(c), h * GDN_DV:(h + 1) * GDN_DV])
            o_ref[_chunk_rows(c), h * GDN_DV:(h + 1) * GDN_DV] = o.astype(o_ref.dtype)
    for h in H:
        state_ref[h] = states[h]


def _gdn(x3, h3, wabt, pcol, prow, normw):
    bsz, seq, _ = x3.shape
    L = min(GDN_TILE, seq)
    const = lambda shape: pl.BlockSpec(shape, lambda b, c: (0,) * len(shape))
    return pl.pallas_call(
        _gdn_kernel,
        grid=(bsz, seq // L),
        in_specs=[pl.BlockSpec((None, L, D_MODEL), lambda b, c: (b, c, 0)),
                  pl.BlockSpec((None, L, COL_W), lambda b, c: (b, c, COL_GDN)),
                  pl.BlockSpec((None, L, D_MODEL), lambda b, c: (b, c, COL_GDN_Z)),
                  pl.BlockSpec((None, L, LANE), lambda b, c: (b, c, COL_GDN_AB)),
                  const(wabt.shape), const(pcol.shape), const(prow.shape), const(normw.shape)],
        out_specs=pl.BlockSpec((None, L, D_MODEL), lambda b, c: (b, c, 0)),
        out_shape=jax.ShapeDtypeStruct((bsz, seq, GDN_HEADS * GDN_DV), BF16),
        scratch_shapes=[pltpu.VMEM((GDN_HEADS, GDN_DK, GDN_DV), F32)],
        compiler_params=_params(("parallel", "arbitrary")),
        name="gdn",
    )(x3, h3, h3, h3, wabt, pcol, prow, normw)


def _ssd_kernel(x_ref, xbcz_ref, dt_ref, wdtt_ref, pcol_ref, prow_ref, dskip_ref, normw_ref, o_ref, state_ref):
    L = xbcz_ref.shape[0]
    C = CHUNK
    nch = L // C
    G = range(SSD_GROUPS)
    R = range(SSD_HPG)
    gw = SSD_HPG * SSD_HEADDIM
    b_off, c_off = SSD_INNER, SSD_INNER + SSD_GROUPS * SSD_STATE

    @pl.when(pl.program_id(1) == 0)
    def _():
        state_ref[...] = jnp.zeros(state_ref.shape, F32)

    causal, _ = _tri_masks(C)
    dt_col = _softplus(dt_ref[...] + pcol_ref[1:2, :])
    la_col = -jnp.exp(pcol_ref[0:1, :]) * dt_col
    dt_t = _mm_nt(wdtt_ref[...], x_ref[...])
    la_row = -jnp.exp(prow_ref[:, 0:1]) * _softplus(dt_t + prow_ref[:, 1:2])
    cum_col, cum_row = _chunk_cumsums(la_col, la_row, nch)

    PG = [(c, g) for c in range(nch) for g in G]
    bm = {(c, g): xbcz_ref[_chunk_rows(c), b_off + g * SSD_STATE:b_off + (g + 1) * SSD_STATE] for c, g in PG}
    cm = {(c, g): xbcz_ref[_chunk_rows(c), c_off + g * SSD_STATE:c_off + (g + 1) * SSD_STATE] for c, g in PG}
    scores = {p: _mm_nt(cm[p], bm[p]) for p in PG}
    PH = [(c, g, r) for c in range(nch) for g in G for r in R]
    head = lambda g, r: g * SSD_HPG + r
    xs = {(c, g, r): xbcz_ref[_chunk_rows(c), head(g, r) * SSD_HEADDIM:(head(g, r) + 1) * SSD_HEADDIM]
          for c, g, r in PH}
    cum_c = {(c, g, r): cum_col[c][:, head(g, r):head(g, r) + 1] for c, g, r in PH}
    decay = {(c, g, r): _decay_matrix(cum_c[c, g, r], cum_row[c][head(g, r):head(g, r) + 1, :], causal)
             for c, g, r in PH}
    v = {(c, g, r): xs[c, g, r] * dt_col[_chunk_rows(c), head(g, r):head(g, r) + 1] for c, g, r in PH}
    y_intra = {(c, g, r): _mm(scores[c, g] * decay[c, g, r], v[c, g, r]) for c, g, r in PH}
    last = {p: cum_c[p][C - 1:C, :] for p in PH}
    v_k = {p: v[p] * jnp.exp(last[p] - cum_c[p]) for p in PH}
    k_v = {(c, g): _mm_tn(bm[c, g], jnp.concatenate([v_k[c, g, r] for r in R], axis=-1)) for c, g in PG}
    lane_head = lax.broadcasted_iota(jnp.int32, (1, gw), 1) // SSD_HEADDIM
    spread = lambda vals: functools.reduce(
        lambda acc, r: jnp.where(lane_head == r, vals[r], acc), R[1:],
        jnp.broadcast_to(vals[0], (vals[0].shape[0], gw)))

    states = [state_ref[g] for g in G]
    for c in range(nch):
        y_state = [_mm(cm[c, g], states[g]) for g in G]
        states = [states[g] * spread([jnp.exp(last[c, g, r]) for r in R]) + k_v[c, g] for g in G]
        for g in G:
            e_cum = spread([jnp.exp(cum_c[c, g, r]) for r in R])
            x_g = xbcz_ref[_chunk_rows(c), g * gw:(g + 1) * gw]
            y = (jnp.concatenate([y_intra[c, g, r] for r in R], axis=-1) + y_state[g] * e_cum
                 + dskip_ref[:, g * gw:(g + 1) * gw] * x_g)
            y = y * _silu(xbcz_ref[_chunk_rows(c), SSD_XBC + g * gw:SSD_XBC + (g + 1) * gw])
            y = y * lax.rsqrt(jnp.mean(jnp.square(y), -1, keepdims=True) + RMS_EPS)
            o_ref[_chunk_rows(c), g * gw:(g + 1) * gw] = (y * normw_ref[:, g * gw:(g + 1) * gw]).astype(o_ref.dtype)
    for g in G:
        state_ref[g] = states[g]


def _ssd(x3, h3, wdtt, pcol, prow, dskip, normw):
    bsz, seq, _ = x3.shape
    L = min(SSD_TILE, seq)
    const = lambda shape: pl.BlockSpec(shape, lambda b, c: (0,) * len(shape))
    return pl.pallas_call(
        _ssd_kernel,
        grid=(bsz, seq // L),
        in_specs=[pl.BlockSpec((None, L, D_MODEL), lambda b, c: (b, c, 0)),
                  pl.BlockSpec((None, L, COL_W), lambda b, c: (b, c, COL_SSD)),
                  pl.BlockSpec((None, L, LANE), lambda b, c: (b, c, COL_SSD_DT)),
                  const(wdtt.shape), const(pcol.shape), const(prow.shape), const(dskip.shape),
                  const(normw.shape)],
        out_specs=pl.BlockSpec((None, L, D_MODEL), lambda b, c: (b, c, 0)),
        out_shape=jax.ShapeDtypeStruct((bsz, seq, SSD_INNER), BF16),
        scratch_shapes=[pltpu.VMEM((SSD_GROUPS, SSD_STATE, SSD_HPG * SSD_HEADDIM), F32)],
        compiler_params=_params(("parallel", "arbitrary")),
        name="ssd",
    )(x3, h3, h3, wdtt, pcol, prow, dskip, normw)


def _ret_kernel(qkvg_ref, cos_ref, sin_ref, o_ref, state_ref):
    L = qkvg_ref.shape[0]
    C = CHUNK
    nch = L // C
    H = range(RET_HEADS)

    @pl.when(pl.program_id(1) == 0)
    def _():
        state_ref[...] = jnp.zeros(state_ref.shape, F32)

    causal, _ = _tri_masks(C)
    row = lax.broadcasted_iota(jnp.int32, (C, C), 0)
    col = lax.broadcasted_iota(jnp.int32, (C, C), 1)
    dist = (row - col).astype(F32)
    pos = lax.broadcasted_iota(jnp.int32, (C, 1), 0).astype(F32)
    log_gamma = [math.log1p(-2.0 ** (-5.0 - h)) for h in H]
    decay = [jnp.exp(jnp.where(causal, dist * log_gamma[h], -jnp.inf)) for h in H]
    e_cum = [jnp.exp((pos + 1.0) * log_gamma[h]) for h in H]
    e_rest = [jnp.exp((C - 1.0 - pos) * log_gamma[h]) for h in H]
    k_off = RET_HEADS * RET_DK
    v_off = 2 * RET_HEADS * RET_DK
    g_off = v_off + RET_HEADS * RET_DV

    P = [(c, h) for c in range(nch) for h in H]
    cos2 = {c: cos_ref[_chunk_rows(c), :] for c in range(nch)}
    sin2 = {c: sin_ref[_chunk_rows(c), :] for c in range(nch)}
    rot = lambda t, c: t * cos2[c] + pltpu.roll(t, RET_DK // 2, 1) * sin2[c]
    q = {(c, h): rot(qkvg_ref[_chunk_rows(c), h * RET_DK:(h + 1) * RET_DK], c) for c, h in P}
    k = {(c, h): rot(qkvg_ref[_chunk_rows(c), k_off + h * RET_DK:k_off + (h + 1) * RET_DK], c) * RET_DK ** -0.5
         for c, h in P}
    v = {(c, h): qkvg_ref[_chunk_rows(c), v_off + h * RET_DV:v_off + (h + 1) * RET_DV] for c, h in P}
    scores = {p: _mm_nt(q[p], k[p]) for p in P}
    y_intra = {(c, h): _mm(scores[c, h] * decay[h], v[c, h]) for c, h in P}
    k_v = {(c, h): _mm_tn(k[c, h], v[c, h] * e_rest[h]) for c, h in P}

    states = [state_ref[h] for h in H]
    for c in range(nch):
        y_state = [_mm(q[c, h], states[h]) for h in H]
        states = [states[h] * math.exp(C * log_gamma[h]) + k_v[c, h] for h in H]
        for h in H:
            o = y_intra[c, h] + y_state[h] * e_cum[h]
            mu = jnp.mean(o, -1, keepdims=True)
            var = jnp.mean(jnp.square(o - mu), -1, keepdims=True)
            o = (o - mu) * lax.rsqrt(var + LN_EPS)
            o = _silu(qkvg_ref[_chunk_rows(c), g_off + h * RET_DV:g_off + (h + 1) * RET_DV]) * o
            o_ref[_chunk_rows(c), h * RET_DV:(h + 1) * RET_DV] = o.astype(o_ref.dtype)
    for h in H:
        state_ref[h] = states[h]


def _ret(h3, cos2, sin2):
    bsz, seq, _ = h3.shape
    L = min(RET_TILE, seq)
    return pl.pallas_call(
        _ret_kernel,
        grid=(bsz, seq // L),
        in_specs=[pl.BlockSpec((None, L, COL_W), lambda b, c: (b, c, COL_RET)),
                  pl.BlockSpec((L, RET_DK), lambda b, c: (c, 0)),
                  pl.BlockSpec((L, RET_DK), lambda b, c: (c, 0))],
        out_specs=pl.BlockSpec((None, L, D_MODEL), lambda b, c: (b, c, 0)),
        out_shape=jax.ShapeDtypeStruct((bsz, seq, RET_HEADS * RET_DV), BF16),
        scratch_shapes=[pltpu.VMEM((RET_HEADS, RET_DK, RET_DV), F32)],
        compiler_params=_params(("parallel", "arbitrary")),
        name="ret",
    )(h3, cos2, sin2)


def _merge_kernel(x_ref, gate_ref, yg_ref, ys_ref, yr_ref, wg_ref, ws_ref, wr_ref, wo_ref, g_ref, b_ref,
                  o_ref):
    d = D_MODEL
    merged = (jax.nn.sigmoid(gate_ref[:, 0:d]) * _mm(yg_ref[...], wg_ref[...])
              + jax.nn.sigmoid(gate_ref[:, d:2 * d]) * _mm(ys_ref[...], ws_ref[...])
              + jax.nn.sigmoid(gate_ref[:, 2 * d:3 * d]) * _mm(yr_ref[...], wr_ref[...]))
    mix = _mm(merged, wo_ref[...])
    o_ref[...] = _layer_norm(DN_ALPHA * x_ref[...] + mix, g_ref[...], b_ref[...])


def _merge(x2, h2, yg, ys, yr, wg, ws, wr, wo, g, b, tm=512):
    t = x2.shape[0]
    tm = min(tm, t)
    row = lambda w: pl.BlockSpec((tm, w), lambda i: (i, 0))
    const = lambda shape: pl.BlockSpec(shape, lambda i: (0,) * len(shape))
    return pl.pallas_call(
        _merge_kernel,
        grid=(t // tm,),
        in_specs=[row(D_MODEL), pl.BlockSpec((tm, COL_W), lambda i: (i, COL_GATE)),
                  row(D_MODEL), row(D_MODEL), row(D_MODEL),
                  const(wg.shape), const(ws.shape), const(wr.shape), const(wo.shape),
                  const(g.shape), const(b.shape)],
        out_specs=row(D_MODEL),
        out_shape=jax.ShapeDtypeStruct((t, D_MODEL), F32),
        compiler_params=_params(("parallel",)),
        name="merge",
    )(x2, h2, yg, ys, yr, wg, ws, wr, wo, g, b)


def _xattn_kernel(x_ref, kv_ref, wq_ref, wo_ref, g_ref, b_ref, o_ref):
    x = x_ref[...]
    q = _mm(x, wq_ref[...])
    outs = []
    for h in range(XA_HEADS):
        k = kv_ref[:, h * XA_DH:(h + 1) * XA_DH]
        v = kv_ref[:, D_MODEL + h * XA_DH:D_MODEL + (h + 1) * XA_DH]
        s = _mm_nt(q[:, h * XA_DH:(h + 1) * XA_DH], k) * XA_DH ** -0.5
        e = jnp.exp(s - jnp.max(s, -1, keepdims=True))
        outs.append(_mm(e / jnp.sum(e, -1, keepdims=True), v))
    xa = _mm(jnp.concatenate(outs, axis=-1), wo_ref[...])
    o_ref[...] = _layer_norm(DN_ALPHA * x + xa, g_ref[...], b_ref[...])


def _xattn(x3, kv3, wq, wo, g, b, tm=512):
    bsz, seq, _ = x3.shape
    tm = min(tm, seq)
    const = lambda shape: pl.BlockSpec(shape, lambda bb, i: (0,) * len(shape))
    return pl.pallas_call(
        _xattn_kernel,
        grid=(bsz, seq // tm),
        in_specs=[pl.BlockSpec((None, tm, D_MODEL), lambda bb, i: (bb, i, 0)),
                  pl.BlockSpec((None, MEM_LEN, 2 * D_MODEL), lambda bb, i: (bb, 0, 0)),
                  const(wq.shape), const(wo.shape), const(g.shape), const(b.shape)],
        out_specs=pl.BlockSpec((None, tm, D_MODEL), lambda bb, i: (bb, i, 0)),
        out_shape=jax.ShapeDtypeStruct((bsz, seq, D_MODEL), F32),
        compiler_params=_params(("parallel", "parallel")),
        name="xattn",
    )(x3, kv3, wq, wo, g, b)


def _top2_sum(a, b, c, d):
    return jnp.maximum(jnp.maximum(a, b) + jnp.maximum(c, d), jnp.maximum(a + b, c + d))


def _router_kernel(x_ref, wrt_ref, br_ref, e_ref, w_ref, rank_ref, cnt_ref, carry_ref):
    tm = x_ref.shape[0]
    epg = EXPERTS_PER_GROUP

    @pl.when(pl.program_id(0) == 0)
    def _():
        carry_ref[...] = jnp.zeros(carry_ref.shape, F32)

    scores = jax.nn.sigmoid(_mm_nt(wrt_ref[...], x_ref[...]))
    sel = scores + br_ref[...]
    rows = [sel[i:i + 1, :] for i in range(N_EXPERTS)]
    srow = [scores[i:i + 1, :] for i in range(N_EXPERTS)]
    gscore = [_top2_sum(*rows[g * epg:(g + 1) * epg]) for g in range(N_EXPERT_GROUPS)]
    best, gidx = gscore[0], jnp.zeros((1, tm), jnp.int32)
    for g in range(1, N_EXPERT_GROUPS):
        better = gscore[g] > best
        best = jnp.where(better, gscore[g], best)
        gidx = jnp.where(better, g, gidx)
    pick = lambda vals, j: functools.reduce(
        lambda acc, g: jnp.where(gidx == g, vals[g * epg + j], acc), range(1, N_EXPERT_GROUPS), vals[j])
    ing = [pick(rows, j) for j in range(epg)]
    ins = [pick(srow, j) for j in range(epg)]
    v0, l0, s0 = ing[0], jnp.zeros((1, tm), jnp.int32), ins[0]
    for j in range(1, epg):
        better = ing[j] > v0
        v0 = jnp.where(better, ing[j], v0)
        l0 = jnp.where(better, j, l0)
        s0 = jnp.where(better, ins[j], s0)
    v1 = jnp.full((1, tm), -jnp.inf, F32)
    l1 = jnp.zeros((1, tm), jnp.int32)
    s1 = jnp.zeros((1, tm), F32)
    for j in range(epg):
        better = (ing[j] > v1) & (l0 != j)
        v1 = jnp.where(better, ing[j], v1)
        l1 = jnp.where(better, j, l1)
        s1 = jnp.where(better, ins[j], s1)
    e0 = gidx * epg + l0
    e1 = gidx * epg + l1
    tot = s0 + s1
    e_ref[0:1, :] = e0
    e_ref[1:2, :] = e1
    w_ref[0:1, :] = s0 / tot
    w_ref[1:2, :] = s1 / tot
    eid = lax.broadcasted_iota(jnp.int32, (N_EXPERTS, tm), 0)
    oh0 = (eid == e0).astype(F32)
    oh1 = (eid == e1).astype(F32)
    ti = lax.broadcasted_iota(jnp.int32, (tm, tm), 0)
    tj = lax.broadcasted_iota(jnp.int32, (tm, tm), 1)
    before = _mm(oh0 + oh1, (ti < tj).astype(F32)) + carry_ref[...]
    rank_ref[0:1, :] = jnp.sum(oh0 * before, 0, keepdims=True).astype(jnp.int32)
    rank_ref[1:2, :] = jnp.sum(oh1 * before, 0, keepdims=True).astype(jnp.int32)
    carry_ref[...] = carry_ref[...] + jnp.sum(oh0 + oh1, 1, keepdims=True)
    cnt_ref[...] = carry_ref[...].astype(jnp.int32)


def _router(x2, wrt, br, tm=512):
    t = x2.shape[0]
    tm = min(tm, t)
    const = lambda shape: pl.BlockSpec(shape, lambda i: (0,) * len(shape))
    tok = pl.BlockSpec((TOP_K, tm), lambda i: (0, i))
    return pl.pallas_call(
        _router_kernel,
        grid=(t // tm,),
        in_specs=[pl.BlockSpec((tm, D_MODEL), lambda i: (i, 0)), const(wrt.shape), const(br.shape)],
        out_specs=[tok, tok, tok, const((N_EXPERTS, 1))],
        out_shape=[jax.ShapeDtypeStruct((TOP_K, t), jnp.int32), jax.ShapeDtypeStruct((TOP_K, t), F32),
                   jax.ShapeDtypeStruct((TOP_K, t), jnp.int32),
                   jax.ShapeDtypeStruct((N_EXPERTS, 1), jnp.int32)],
        scratch_shapes=[pltpu.VMEM((N_EXPERTS, 1), F32)],
        compiler_params=_params(("arbitrary",)),
        name="router",
    )(x2, wrt, br)


def _ffn_kernel(be_ref, nused_ref, x_ref, wgu_ref, wd_ref, o_ref):
    i = pl.program_id(0)

    @pl.when(i < nused_ref[0])
    def _():
        gu = _mm(x_ref[...], wgu_ref[...])
        hid = _silu(gu[:, 0:D_EXPERT]) * gu[:, D_EXPERT:2 * D_EXPERT]
        o_ref[...] = _mm(hid, wd_ref[...])

    @pl.when(i >= nused_ref[0])
    def _():
        o_ref[...] = jnp.zeros(o_ref.shape, o_ref.dtype)


def _ffn(block_expert, n_used, xb, wgu, wd):
    npad = xb.shape[0]
    return pl.pallas_call(
        _ffn_kernel,
        grid_spec=pltpu.PrefetchScalarGridSpec(
            num_scalar_prefetch=2,
            grid=(npad // MOE_BLK,),
            in_specs=[pl.BlockSpec((MOE_BLK, D_MODEL), lambda i, be, nu: (i, 0)),
                      pl.BlockSpec((None, D_MODEL, 2 * D_EXPERT), lambda i, be, nu: (be[i], 0, 0)),
                      pl.BlockSpec((None, D_EXPERT, D_MODEL), lambda i, be, nu: (be[i], 0, 0))],
            out_specs=pl.BlockSpec((MOE_BLK, D_MODEL), lambda i, be, nu: (i, 0))),
        out_shape=jax.ShapeDtypeStruct((npad, D_MODEL), F32),
        compiler_params=_params(("arbitrary",)),
        name="ffn",
    )(block_expert, n_used, xb, wgu, wd)


def _combine_kernel(x_ref, y0_ref, y1_ref, w_ref, g_ref, b_ref, o_ref):
    ff = y0_ref[...] * w_ref[:, 0:1] + y1_ref[...] * w_ref[:, 1:2]
    o_ref[...] = _layer_norm(DN_ALPHA * x_ref[...] + ff, g_ref[...], b_ref[...])


def _combine(x2, y0, y1, w, g, b, tm=512):
    t = x2.shape[0]
    tm = min(tm, t)
    row = lambda w_: pl.BlockSpec((tm, w_), lambda i: (i, 0))
    const = lambda shape: pl.BlockSpec(shape, lambda i: (0,) * len(shape))
    return pl.pallas_call(
        _combine_kernel,
        grid=(t // tm,),
        in_specs=[row(D_MODEL), row(D_MODEL), row(D_MODEL), row(TOP_K), const(g.shape), const(b.shape)],
        out_specs=row(D_MODEL),
        out_shape=jax.ShapeDtypeStruct((t, D_MODEL), F32),
        compiler_params=_params(("parallel",)),
        name="combine",
    )(x2, y0, y1, w, g, b)


def _moe(x2, wrt, br, wgu, wd, ln_g, ln_b):
    t = x2.shape[0]
    expert, gate, rank, counts = _router(x2, wrt, br)
    counts = counts[:, 0]
    padded = (counts + MOE_BLK - 1) // MOE_BLK * MOE_BLK
    pend = jnp.cumsum(padded)
    pstart = pend - padded
    n_blocks = (t * TOP_K + N_EXPERTS * (MOE_BLK - 1) + MOE_BLK - 1) // MOE_BLK
    npad = n_blocks * MOE_BLK
    dest = pstart[expert] + rank
    tok = jnp.broadcast_to(jnp.arange(t, dtype=jnp.int32)[None, :], (TOP_K, t))
    buf_tok = jnp.zeros((npad,), jnp.int32).at[dest.reshape(-1)].set(tok.reshape(-1))
    block_expert = jnp.minimum(
        jnp.searchsorted(pend, jnp.arange(n_blocks, dtype=jnp.int32) * MOE_BLK, side='right'),
        N_EXPERTS - 1).astype(jnp.int32)
    n_used = (pend[-1:] // MOE_BLK).astype(jnp.int32)
    xb = x2.astype(BF16)[buf_tok]
    yb = _ffn(block_expert, n_used, xb, wgu, wd)
    return _combine(x2, yb[dest[0]], yb[dest[1]], gate.T, ln_g, ln_b)


def _prep_weights(mix_w_in, gdn_conv_w, gdn_a_log, gdn_dt_bias, ssd_conv_w, ssd_conv_b, ssd_a_log, ssd_dt_bias,
                  ssd_d, moe_w_gate, moe_w_up, xa_w_k, xa_w_v, router_w, router_b):
    nl = mix_w_in.shape[0]
    off = [0]
    for s in (GDN_QKV, GDN_HEADS * GDN_DV, GDN_HEADS, GDN_HEADS, SSD_XBC, SSD_INNER, SSD_HEADS,
              RET_HEADS * RET_DK, RET_HEADS * RET_DK, RET_HEADS * RET_DV, RET_HEADS * RET_DV,
              N_BRANCH * D_MODEL):
        off.append(off[-1] + s)
    sl = lambda i, j=None: mix_w_in[:, :, off[i]:off[(i if j is None else j) + 1]]
    zeros = lambda n: jnp.zeros((nl, D_MODEL, n), mix_w_in.dtype)
    w_all = jnp.concatenate([
        sl(11),
        sl(0),
        sl(7, 10),
        sl(4, 5),
        sl(1),
        sl(2, 3), zeros(LANE - 2 * GDN_HEADS),
        sl(6), zeros(LANE - SSD_HEADS),
        zeros(PROJ_N - (COL_SSD_DT + 1) * LANE)], axis=-1).astype(BF16)
    wabt = jnp.swapaxes(sl(2, 3), 1, 2).astype(BF16)
    wdtt = jnp.swapaxes(sl(6), 1, 2).astype(BF16)

    def col_params(a_log, dt_bias, lane0):
        n = a_log.shape[1]
        p = jnp.zeros((nl, 2, LANE), F32)
        return p.at[:, 0, lane0:lane0 + n].set(a_log).at[:, 1, lane0:lane0 + n].set(dt_bias)

    def row_params(a_log, dt_bias):
        n = a_log.shape[1]
        p = jnp.zeros((nl, 16, 2), F32)
        return p.at[:, 0:n, 0].set(a_log).at[:, 0:n, 1].set(dt_bias)

    return dict(
        w_all=w_all, wabt=wabt, wdtt=wdtt,
        gdn_pcol=col_params(gdn_a_log, gdn_dt_bias, 0), gdn_prow=row_params(gdn_a_log, gdn_dt_bias),
        ssd_pcol=col_params(ssd_a_log, ssd_dt_bias, 0), ssd_prow=row_params(ssd_a_log, ssd_dt_bias),
        ssd_dskip=jnp.repeat(ssd_d, SSD_HEADDIM, axis=-1)[:, None, :],
        wgu=jnp.concatenate([moe_w_gate, moe_w_up], axis=-1).astype(BF16),
        wkv=jnp.concatenate([xa_w_k, xa_w_v], axis=-1).astype(BF16),
        conv_w=jnp.zeros((nl, CONV_K, PROJ_N), F32)
        .at[:, :, COL_GDN * COL_W:COL_GDN * COL_W + GDN_QKV].set(gdn_conv_w)
        .at[:, :, COL_SSD * COL_W:COL_SSD * COL_W + SSD_XBC].set(ssd_conv_w),
        conv_b=jnp.zeros((nl, 1, PROJ_N), F32)
        .at[:, 0, COL_SSD * COL_W:COL_SSD * COL_W + SSD_XBC].set(ssd_conv_b),
        wrt=router_w.T.astype(BF16), br=router_b[:, None],
    )


def kernel(x, mem, mix_w_in, gdn_conv_w, gdn_a_log, gdn_dt_bias, gdn_norm_w, ssd_conv_w, ssd_conv_b, ssd_a_log, ssd_dt_bias, ssd_d, ssd_norm_w, w_proj_gdn, w_proj_ssd, w_proj_ret, mix_w_out, ln_mix_g, ln_mix_b, xa_w_q, xa_w_k, xa_w_v, xa_w_o, ln_xa_g, ln_xa_b, router_w, router_b, moe_w_gate, moe_w_up, moe_w_down, ln_moe_g, ln_moe_b):
    bsz, seq, d = x.shape
    t = bsz * seq
    p = _prep_weights(mix_w_in, gdn_conv_w, gdn_a_log, gdn_dt_bias, ssd_conv_w, ssd_conv_b, ssd_a_log,
                      ssd_dt_bias, ssd_d, moe_w_gate, moe_w_up, xa_w_k, xa_w_v, router_w, router_b)
    bf = lambda w: w.astype(BF16)
    wpg, wps, wpr, wout = bf(w_proj_gdn), bf(w_proj_ssd), bf(w_proj_ret), bf(mix_w_out)
    wq, wo, wd = bf(xa_w_q), bf(xa_w_o), bf(moe_w_down)
    vec = lambda v, l: v[l][None, :]

    pos = jnp.arange(seq, dtype=F32)
    inv_freq = ROPE_BASE ** (-jnp.arange(0, RET_DK, 2, dtype=F32) / RET_DK)
    ang = pos[:, None] * inv_freq[None, :]
    cos, sin = jnp.cos(ang), jnp.sin(ang)
    cos2 = jnp.concatenate([cos, cos], axis=-1)
    sin2 = jnp.concatenate([-sin, sin], axis=-1)
    mem2 = mem.reshape(bsz * mem.shape[1], d)

    x2 = x.reshape(t, d)
    for l in range(DEPTH):
        h2 = _proj(x2, p["w_all"][l], p["conv_w"][l], p["conv_b"][l], seq)
        x3, h3 = x2.reshape(bsz, seq, d), h2.reshape(bsz, seq, PROJ_N)
        yg = _gdn(x3, h3, p["wabt"][l], p["gdn_pcol"][l], p["gdn_prow"][l], vec(gdn_norm_w, l))
        ys = _ssd(x3, h3, p["wdtt"][l], p["ssd_pcol"][l], p["ssd_prow"][l], p["ssd_dskip"][l],
                  vec(ssd_norm_w, l))
        yr = _ret(h3, cos2, sin2)
        x2 = _merge(x2, h2, yg.reshape(t, d), ys.reshape(t, d), yr.reshape(t, d), wpg[l], wps[l], wpr[l],
                    wout[l], vec(ln_mix_g, l), vec(ln_mix_b, l))
        kv = _matmul(mem2, p["wkv"][l], BF16, 512, 1024)
        x2 = _xattn(x2.reshape(bsz, seq, d), kv.reshape(bsz, mem.shape[1], 2 * d), wq[l], wo[l],
                    vec(ln_xa_g, l), vec(ln_xa_b, l)).reshape(t, d)
        x2 = _moe(x2, p["wrt"], p["br"], p["wgu"][l], wd[l], vec(ln_moe_g, l), vec(ln_moe_b, l))
    return x2.reshape(bsz, seq, d)
```

```python
import functools
import math

import jax
import jax.numpy as jnp
from jax import lax
from jax.experimental import pallas as pl
from jax.experimental.pallas import tpu as pltpu

F32 = jnp.float32
BF16 = jnp.bfloat16

D_MODEL = 1024
DEPTH = 4
CHUNK = 64
CONV_K = 4
GDN_HEADS = 8
GDN_DK = 128
GDN_DV = 128
SSD_INNER = D_MODEL
SSD_HEADDIM = 64
SSD_HEADS = SSD_INNER // SSD_HEADDIM
SSD_GROUPS = 4
SSD_HPG = SSD_HEADS // SSD_GROUPS
SSD_STATE = 128
RET_HEADS = 4
RET_DK = 128
RET_DV = 256
ROPE_BASE = 10000.0
N_BRANCH = 3
MEM_LEN = 256
XA_HEADS = 4
XA_DH = D_MODEL // XA_HEADS
N_EXPERTS = 16
N_EXPERT_GROUPS = 4
EXPERTS_PER_GROUP = N_EXPERTS // N_EXPERT_GROUPS
TOP_K = 2
D_EXPERT = 512
DN_ALPHA = (2 * DEPTH) ** 0.25
LN_EPS = 1e-5
RMS_EPS = 1e-6

GDN_QKV = 2 * GDN_HEADS * GDN_DK + GDN_HEADS * GDN_DV
SSD_XBC = SSD_INNER + 2 * SSD_GROUPS * SSD_STATE

LANE = 128
COL_W = 3072
COL_GATE = 0
COL_GDN = 1
COL_RET = 2
COL_SSD = 3
COL_GDN_Z = 12
COL_GDN_AB = 104
COL_SSD_DT = 105
PROJ_N = 13824
PROJ_TM = 1024
PROJ_TN = 1536
PROJ_HALO = 16
PROJ_CONV_TILES = (COL_GDN * COL_W // PROJ_TN, COL_GDN * COL_W // PROJ_TN + 1, COL_SSD * COL_W // PROJ_TN)
PROJ_CONV_PART_TILE = COL_SSD * COL_W // PROJ_TN + 1
PROJ_CONV_PART_COLS = SSD_XBC - PROJ_TN
PROJ_VMEM_LIMIT = 56 * 1024 * 1024
GDN_TILE = 256
SSD_TILE = 256
RET_TILE = 256

MOE_BLK = 512
VMEM_LIMIT = 48 * 1024 * 1024


def _mm(a, b):
    return jnp.dot(a.astype(BF16), b.astype(BF16), preferred_element_type=F32)


def _mm_nt(a, b):
    return lax.dot_general(a.astype(BF16), b.astype(BF16), (((1,), (1,)), ((), ())),
                           preferred_element_type=F32)


def _mm_tn(a, b):
    return lax.dot_general(a.astype(BF16), b.astype(BF16), (((0,), (0,)), ((), ())),
                           preferred_element_type=F32)


def _mm_f32(a, b):
    return jnp.dot(a, b, preferred_element_type=F32, precision=lax.Precision.HIGHEST)


def _silu(x):
    hx = 0.5 * x
    return hx + hx * jnp.tanh(hx)


def _softplus(x):
    return jnp.maximum(x, 0.0) + jnp.log1p(jnp.exp(-jnp.abs(x)))


def _layer_norm(v, g, b):
    mu = jnp.mean(v, -1, keepdims=True)
    var = jnp.mean(jnp.square(v - mu), -1, keepdims=True)
    return (v - mu) * lax.rsqrt(var + LN_EPS) * g + b


def _params(sem):
    return pltpu.CompilerParams(dimension_semantics=sem, vmem_limit_bytes=VMEM_LIMIT)


def _matmul_kernel(x_ref, w_ref, o_ref):
    o_ref[...] = _mm(x_ref[...], w_ref[...]).astype(o_ref.dtype)


def _matmul(x, w, out_dtype, tm, tn):
    m, k = x.shape
    n = w.shape[1]
    tm, tn = min(tm, m), min(tn, n)
    return pl.pallas_call(
        _matmul_kernel,
        grid=(m // tm, n // tn),
        in_specs=[pl.BlockSpec((tm, k), lambda i, j: (i, 0)),
                  pl.BlockSpec((k, tn), lambda i, j: (0, j))],
        out_specs=pl.BlockSpec((tm, tn), lambda i, j: (i, j)),
        out_shape=jax.ShapeDtypeStruct((m, n), out_dtype),
        compiler_params=_params(("parallel", "arbitrary")),
        name="matmul",
    )(x, w)


def _proj_kernel(x_ref, xprev_ref, w_ref, cw_ref, cb_ref, o_ref, xs_ref, res_ref, *, tiles_per_seq):
    i, j = pl.program_id(0), pl.program_id(1)
    tm, tn = o_ref.shape

    @pl.when(j == 0)
    def _():
        xs_ref[PROJ_HALO:PROJ_HALO + tm, :] = x_ref[...].astype(BF16)
        starts_sequence = (i % tiles_per_seq) == 0
        xs_ref[0:PROJ_HALO, :] = jnp.where(starts_sequence, 0.0, xprev_ref[...]).astype(BF16)

    r = jnp.dot(xs_ref[...], w_ref[...], preferred_element_type=F32)

    def conv_silu(ncols):
        res_ref[:, 0:ncols] = r[:, 0:ncols]
        acc = r[PROJ_HALO:, 0:ncols] * cw_ref[CONV_K - 1:CONV_K, 0:ncols] + cb_ref[:, 0:ncols]
        for tap in range(CONV_K - 1):
            off = PROJ_HALO - (CONV_K - 1) + tap
            acc = acc + res_ref[off:off + tm, 0:ncols] * cw_ref[tap:tap + 1, 0:ncols]
        return _silu(acc)

    conv_full = (j == PROJ_CONV_TILES[0]) | (j == PROJ_CONV_TILES[1]) | (j == PROJ_CONV_TILES[2])
    conv_part = j == PROJ_CONV_PART_TILE

    @pl.when(conv_full)
    def _():
        o_ref[...] = conv_silu(tn)

    @pl.when(conv_part)
    def _():
        o_ref[:, 0:PROJ_CONV_PART_COLS] = conv_silu(PROJ_CONV_PART_COLS)
        o_ref[:, PROJ_CONV_PART_COLS:] = r[PROJ_HALO:, PROJ_CONV_PART_COLS:]

    @pl.when(jnp.logical_not(conv_full | conv_part))
    def _():
        o_ref[...] = r[PROJ_HALO:, :]


def _proj(x2, w, cw, cb, seq):
    t, d = x2.shape
    tm = min(PROJ_TM, seq)
    tn = PROJ_TN
    halo_blocks = tm // PROJ_HALO
    return pl.pallas_call(
        functools.partial(_proj_kernel, tiles_per_seq=seq // tm),
        grid=(t // tm, PROJ_N // tn),
        in_specs=[pl.BlockSpec((tm, d), lambda i, j: (i, 0)),
                  pl.BlockSpec((PROJ_HALO, d), lambda i, j: (jnp.maximum(i * halo_blocks - 1, 0), 0)),
                  pl.BlockSpec((d, tn), lambda i, j: (0, j)),
                  pl.BlockSpec((CONV_K, tn), lambda i, j: (0, j)),
                  pl.BlockSpec((1, tn), lambda i, j: (0, j))],
        out_specs=pl.BlockSpec((tm, tn), lambda i, j: (i, j)),
        out_shape=jax.ShapeDtypeStruct((t, PROJ_N), F32),
        scratch_shapes=[pltpu.VMEM((PROJ_HALO + tm, d), BF16),
                        pltpu.VMEM((PROJ_HALO + tm, tn), F32)],
        compiler_params=pltpu.CompilerParams(dimension_semantics=("parallel", "arbitrary"),
                                             vmem_limit_bytes=PROJ_VMEM_LIMIT),
        name="proj",
    )(x2, x2, w, cw, cb)


def _tri_masks(n):
    row = lax.broadcasted_iota(jnp.int32, (n, n), 0)
    col = lax.broadcasted_iota(jnp.int32, (n, n), 1)
    return row >= col, row > col


def _upper_ones(n):
    row = lax.broadcasted_iota(jnp.int32, (n, n), 0)
    col = lax.broadcasted_iota(jnp.int32, (n, n), 1)
    return (row <= col).astype(F32)


def _decay_matrix(cum_c, cum_r, causal):
    return jnp.exp(jnp.where(causal, cum_c - cum_r, -jnp.inf))


def _chunk_rows(c):
    return slice(c * CHUNK, (c + 1) * CHUNK)


def _chunk_cumsums(la_col, la_row, nch):
    tri = _tri_masks(CHUNK)[0].astype(F32)
    upper = _upper_ones(CHUNK)
    cum_col = [_mm_f32(tri, la_col[_chunk_rows(c), :]) for c in range(nch)]
    cum_row = [_mm_f32(la_row[:, _chunk_rows(c)], upper) for c in range(nch)]
    return cum_col, cum_row


def _gdn_kernel(x_ref, qkv_ref, z_ref, ab_ref, wabt_ref, pcol_ref, prow_ref, normw_ref, o_ref, state_ref):
    L = qkv_ref.shape[0]
    C = CHUNK
    nch = L // C
    H = range(GDN_HEADS)

    @pl.when(pl.program_id(1) == 0)
    def _():
        state_ref[...] = jnp.zeros(state_ref.shape, F32)

    causal, strict = _tri_masks(C)
    eye = (causal & ~strict).astype(F32)

    ab = ab_ref[...]
    la_col = -jnp.exp(pcol_ref[0:1, :]) * _softplus(ab + pcol_ref[1:2, :])
    beta_col = jax.nn.sigmoid(ab)
    ab_t = _mm_nt(wabt_ref[...], x_ref[...])
    la_row = -jnp.exp(prow_ref[:, 0:1]) * _softplus(ab_t + prow_ref[:, 1:2])
    cum_col, cum_row = _chunk_cumsums(la_col, la_row, nch)

    P = [(c, h) for c in range(nch) for h in H]
    I = range(len(P))
    k_off, v_off = GDN_HEADS * GDN_DK, 2 * GDN_HEADS * GDN_DK
    qs = [qkv_ref[_chunk_rows(c), h * GDN_DK:(h + 1) * GDN_DK] for c, h in P]
    ks = [qkv_ref[_chunk_rows(c), k_off + h * GDN_DK:k_off + (h + 1) * GDN_DK] for c, h in P]
    vs = [qkv_ref[_chunk_rows(c), v_off + h * GDN_DV:v_off + (h + 1) * GDN_DV] for c, h in P]
    qs = [q * (lax.rsqrt(jnp.sum(jnp.square(q), -1, keepdims=True) + RMS_EPS) * GDN_DK ** -0.5) for q in qs]
    ks = [k * lax.rsqrt(jnp.sum(jnp.square(k), -1, keepdims=True) + RMS_EPS) for k in ks]
    cum_c = [cum_col[c][:, h:h + 1] for c, h in P]
    beta = [beta_col[_chunk_rows(c), GDN_HEADS + h:GDN_HEADS + h + 1] for c, h in P]
    decay = [_decay_matrix(cum_c[i], cum_row[c][h:h + 1, :], causal) for i, (c, h) in enumerate(P)]
    kb = [ks[i] * beta[i] for i in I]
    vb = [vs[i] * beta[i] for i in I]
    kq_k = [_mm_nt(jnp.concatenate([kb[i], qs[i]], axis=0), ks[i]) for i in I]
    n_pow = [-jnp.where(strict, kq_k[i][0:C] * decay[i], 0.0) for i in I]
    attn = [jnp.where(causal, kq_k[i][C:2 * C] * decay[i], 0.0) for i in I]
    t_mat = [eye + n_pow[i] for i in I]
    n_pow = [_mm(n_pow[i], n_pow[i]) for i in I]
    for _ in range(int(math.log2(C)) - 2):
        both = [_mm(jnp.concatenate([t_mat[i], n_pow[i]], axis=0), n_pow[i]) for i in I]
        t_mat = [t_mat[i] + both[i][0:C] for i in I]
        n_pow = [both[i][C:2 * C] for i in I]
    upd = [_mm(t_mat[i], n_pow[i]) for i in I]
    t_mat = [t_mat[i] + upd[i] for i in I]
    e_cum = [jnp.exp(cum_c[i]) for i in I]
    uw = [_mm(t_mat[i], jnp.concatenate([vb[i], kb[i] * e_cum[i]], axis=-1)) for i in I]
    u = [uw[i][:, 0:GDN_DV] for i in I]
    w = [uw[i][:, GDN_DV:] for i in I]
    last = [cum_c[i][C - 1:C, :] for i in I]
    wq = [jnp.concatenate([w[i], qs[i] * e_cum[i]], axis=0) for i in I]
    ak = [jnp.concatenate([attn[i], (ks[i] * jnp.exp(last[i] - cum_c[i])).T], axis=0) for i in I]

    states = [state_ref[h] for h in H]
    outs = []
    for c in range(nch):
        ix = [c * GDN_HEADS + h for h in H]
        wq_s = [_mm(wq[i], states[h]) for h, i in zip(H, ix)]
        v_new = [u[i] - wq_s[h][0:C] for h, i in zip(H, ix)]
        ak_v = [_mm(ak[i], v_new[h]) for h, i in zip(H, ix)]
        states = [states[h] * jnp.exp(last[i]) + ak_v[h][C:] for h, i in zip(H, ix)]
        outs += [wq_s[h][C:2 * C] + ak_v[h][0:C] for h in H]
    for h in H:
        state_ref[h] = states[h]
    scale = [lax.rsqrt(jnp.mean(jnp.square(o), -1, keepdims=True) + RMS_EPS) for o in outs]
    gate = [_silu(z_ref[_chunk_rows(c), h * GDN_DV:(h + 1) * GDN_DV]) for c, h in P]
    for i, (c, h) in enumerate(P):
        o_ref[_chunk_rows(c), h * GDN_DV:(h + 1) * GDN_DV] = (
            outs[i] * scale[i] * normw_ref[...] * gate[i]).astype(o_ref.dtype)


def _gdn(x3, h3, wabt, pcol, prow, normw):
    bsz, seq, _ = x3.shape
    L = min(GDN_TILE, seq)
    const = lambda shape: pl.BlockSpec(shape, lambda b, c: (0,) * len(shape))
    return pl.pallas_call(
        _gdn_kernel,
        grid=(bsz, seq // L),
        in_specs=[pl.BlockSpec((None, L, D_MODEL), lambda b, c: (b, c, 0)),
                  pl.BlockSpec((None, L, COL_W), lambda b, c: (b, c, COL_GDN)),
                  pl.BlockSpec((None, L, D_MODEL), lambda b, c: (b, c, COL_GDN_Z)),
                  pl.BlockSpec((None, L, LANE), lambda b, c: (b, c, COL_GDN_AB)),
                  const(wabt.shape), const(pcol.shape), const(prow.shape), const(normw.shape)],
        out_specs=pl.BlockSpec((None, L, D_MODEL), lambda b, c: (b, c, 0)),
        out_shape=jax.ShapeDtypeStruct((bsz, seq, GDN_HEADS * GDN_DV), BF16),
        scratch_shapes=[pltpu.VMEM((GDN_HEADS, GDN_DK, GDN_DV), F32)],
        compiler_params=_params(("parallel", "arbitrary")),
        name="gdn",
    )(x3, h3, h3, h3, wabt, pcol, prow, normw)


def _ssd_kernel(x_ref, xbcz_ref, dt_ref, wdtt_ref, pcol_ref, prow_ref, dskip_ref, normw_ref, o_ref, state_ref):
    L = xbcz_ref.shape[0]
    C = CHUNK
    nch = L // C
    G = range(SSD_GROUPS)
    R = range(SSD_HPG)
    gw = SSD_HPG * SSD_HEADDIM
    b_off, c_off = SSD_INNER, SSD_INNER + SSD_GROUPS * SSD_STATE

    @pl.when(pl.program_id(1) == 0)
    def _():
        state_ref[...] = jnp.zeros(state_ref.shape, F32)

    causal, _ = _tri_masks(C)
    dt_col = _softplus(dt_ref[...] + pcol_ref[1:2, :])
    la_col = -jnp.exp(pcol_ref[0:1, :]) * dt_col
    dt_t = _mm_nt(wdtt_ref[...], x_ref[...])
    la_row = -jnp.exp(prow_ref[:, 0:1]) * _softplus(dt_t + prow_ref[:, 1:2])
    cum_col, cum_row = _chunk_cumsums(la_col, la_row, nch)

    PG = [(c, g) for c in range(nch) for g in G]
    bm = {(c, g): xbcz_ref[_chunk_rows(c), b_off + g * SSD_STATE:b_off + (g + 1) * SSD_STATE] for c, g in PG}
    cm = {(c, g): xbcz_ref[_chunk_rows(c), c_off + g * SSD_STATE:c_off + (g + 1) * SSD_STATE] for c, g in PG}
    scores = {p: _mm_nt(cm[p], bm[p]) for p in PG}
    PH = [(c, g, r) for c in range(nch) for g in G for r in R]
    head = lambda g, r: g * SSD_HPG + r
    xs = {(c, g, r): xbcz_ref[_chunk_rows(c), head(g, r) * SSD_HEADDIM:(head(g, r) + 1) * SSD_HEADDIM]
          for c, g, r in PH}
    cum_c = {(c, g, r): cum_col[c][:, head(g, r):head(g, r) + 1] for c, g, r in PH}
    decay = {(c, g, r): _decay_matrix(cum_c[c, g, r], cum_row[c][head(g, r):head(g, r) + 1, :], causal)
             for c, g, r in PH}
    v = {(c, g, r): xs[c, g, r] * dt_col[_chunk_rows(c), head(g, r):head(g, r) + 1] for c, g, r in PH}
    y_intra = {(c, g, r): _mm(scores[c, g] * decay[c, g, r], v[c, g, r]) for c, g, r in PH}
    last = {p: cum_c[p][C - 1:C, :] for p in PH}
    v_k = {p: v[p] * jnp.exp(last[p] - cum_c[p]) for p in PH}
    k_v = {(c, g): _mm_tn(bm[c, g], jnp.concatenate([v_k[c, g, r] for r in R], axis=-1)) for c, g in PG}
    lane_head = lax.broadcasted_iota(jnp.int32, (1, gw), 1) // SSD_HEADDIM
    spread = lambda vals: functools.reduce(
        lambda acc, r: jnp.where(lane_head == r, vals[r], acc), R[1:],
        jnp.broadcast_to(vals[0], (vals[0].shape[0], gw)))

    states = [state_ref[g] for g in G]
    y_state = {}
    for c in range(nch):
        for g in G:
            y_state[c, g] = _mm(cm[c, g], states[g])
        states = [states[g] * spread([jnp.exp(last[c, g, r]) for r in R]) + k_v[c, g] for g in G]
    for g in G:
        state_ref[g] = states[g]
    e_cum = {(c, g): spread([jnp.exp(cum_c[c, g, r]) for r in R]) for c, g in PG}
    cols = lambda g: slice(g * gw, (g + 1) * gw)
    y = {(c, g): (jnp.concatenate([y_intra[c, g, r] for r in R], axis=-1) + y_state[c, g] * e_cum[c, g]
                  + dskip_ref[:, cols(g)] * xbcz_ref[_chunk_rows(c), cols(g)]) for c, g in PG}
    y = {(c, g): y[c, g] * _silu(xbcz_ref[_chunk_rows(c), SSD_XBC + g * gw:SSD_XBC + (g + 1) * gw]) for c, g in PG}
    scale = {p: lax.rsqrt(jnp.mean(jnp.square(y[p]), -1, keepdims=True) + RMS_EPS) for p in PG}
    for c, g in PG:
        o_ref[_chunk_rows(c), cols(g)] = (y[c, g] * scale[c, g] * normw_ref[:, cols(g)]).astype(o_ref.dtype)


def _ssd(x3, h3, wdtt, pcol, prow, dskip, normw):
    bsz, seq, _ = x3.shape
    L = min(SSD_TILE, seq)
    const = lambda shape: pl.BlockSpec(shape, lambda b, c: (0,) * len(shape))
    return pl.pallas_call(
        _ssd_kernel,
        grid=(bsz, seq // L),
        in_specs=[pl.BlockSpec((None, L, D_MODEL), lambda b, c: (b, c, 0)),
                  pl.BlockSpec((None, L, COL_W), lambda b, c: (b, c, COL_SSD)),
                  pl.BlockSpec((None, L, LANE), lambda b, c: (b, c, COL_SSD_DT)),
                  const(wdtt.shape), const(pcol.shape), const(prow.shape), const(dskip.shape),
                  const(normw.shape)],
        out_specs=pl.BlockSpec((None, L, D_MODEL), lambda b, c: (b, c, 0)),
        out_shape=jax.ShapeDtypeStruct((bsz, seq, SSD_INNER), BF16),
        scratch_shapes=[pltpu.VMEM((SSD_GROUPS, SSD_STATE, SSD_HPG * SSD_HEADDIM), F32)],
        compiler_params=_params(("parallel", "arbitrary")),
        name="ssd",
    )(x3, h3, h3, wdtt, pcol, prow, dskip, normw)


def _ret_kernel(qkvg_ref, cos_ref, sin_ref, o_ref, state_ref):
    L = qkvg_ref.shape[0]
    C = CHUNK
    nch = L // C
    H = range(RET_HEADS)

    @pl.when(pl.program_id(1) == 0)
    def _():
        state_ref[...] = jnp.zeros(state_ref.shape, F32)

    causal, _ = _tri_masks(C)
    row = lax.broadcasted_iota(jnp.int32, (C, C), 0)
    col = lax.broadcasted_iota(jnp.int32, (C, C), 1)
    dist = (row - col).astype(F32)
    pos = lax.broadcasted_iota(jnp.int32, (C, 1), 0).astype(F32)
    log_gamma = [math.log1p(-2.0 ** (-5.0 - h)) for h in H]
    decay = [jnp.exp(jnp.where(causal, dist * log_gamma[h], -jnp.inf)) for h in H]
    e_cum = [jnp.exp((pos + 1.0) * log_gamma[h]) for h in H]
    e_rest = [jnp.exp((C - 1.0 - pos) * log_gamma[h]) for h in H]
    k_off = RET_HEADS * RET_DK
    v_off = 2 * RET_HEADS * RET_DK
    g_off = v_off + RET_HEADS * RET_DV

    P = [(c, h) for c in range(nch) for h in H]
    cos2 = {c: cos_ref[_chunk_rows(c), :] for c in range(nch)}
    sin2 = {c: sin_ref[_chunk_rows(c), :] for c in range(nch)}
    rot = lambda t, c: t * cos2[c] + pltpu.roll(t, RET_DK // 2, 1) * sin2[c]
    q = {(c, h): rot(qkvg_ref[_chunk_rows(c), h * RET_DK:(h + 1) * RET_DK], c) for c, h in P}
    k = {(c, h): rot(qkvg_ref[_chunk_rows(c), k_off + h * RET_DK:k_off + (h + 1) * RET_DK], c) * RET_DK ** -0.5
         for c, h in P}
    v = {(c, h): qkvg_ref[_chunk_rows(c), v_off + h * RET_DV:v_off + (h + 1) * RET_DV] for c, h in P}
    scores = {p: _mm_nt(q[p], k[p]) for p in P}
    y_intra = {(c, h): _mm(scores[c, h] * decay[h], v[c, h]) for c, h in P}
    k_v = {(c, h): _mm_tn(k[c, h], v[c, h] * e_rest[h]) for c, h in P}

    states = [state_ref[h] for h in H]
    for c in range(nch):
        y_state = [_mm(q[c, h], states[h]) for h in H]
        states = [states[h] * math.exp(C * log_gamma[h]) + k_v[c, h] for h in H]
        for h in H:
            o = y_intra[c, h] + y_state[h] * e_cum[h]
            mu = jnp.mean(o, -1, keepdims=True)
            var = jnp.mean(jnp.square(o - mu), -1, keepdims=True)
            o = (o - mu) * lax.rsqrt(var + LN_EPS)
            o = _silu(qkvg_ref[_chunk_rows(c), g_off + h * RET_DV:g_off + (h + 1) * RET_DV]) * o
            o_ref[_chunk_rows(c), h * RET_DV:(h + 1) * RET_DV] = o.astype(o_ref.dtype)
    for h in H:
        state_ref[h] = states[h]


def _ret(h3, cos2, sin2):
    bsz, seq, _ = h3.shape
    L = min(RET_TILE, seq)
    return pl.pallas_call(
        _ret_kernel,
        grid=(bsz, seq // L),
        in_specs=[pl.BlockSpec((None, L, COL_W), lambda b, c: (b, c, COL_RET)),
                  pl.BlockSpec((L, RET_DK), lambda b, c: (c, 0)),
                  pl.BlockSpec((L, RET_DK), lambda b, c: (c, 0))],
        out_specs=pl.BlockSpec((None, L, D_MODEL), lambda b, c: (b, c, 0)),
        out_shape=jax.ShapeDtypeStruct((bsz, seq, RET_HEADS * RET_DV), BF16),
        scratch_shapes=[pltpu.VMEM((RET_HEADS, RET_DK, RET_DV), F32)],
        compiler_params=_params(("parallel", "arbitrary")),
        name="ret",
    )(h3, cos2, sin2)


def _merge_kernel(x_ref, gate_ref, yg_ref, ys_ref, yr_ref, wg_ref, ws_ref, wr_ref, wo_ref, g_ref, b_ref,
                  o_ref):
    d = D_MODEL
    merged = (jax.nn.sigmoid(gate_ref[:, 0:d]) * _mm(yg_ref[...], wg_ref[...])
              + jax.nn.sigmoid(gate_ref[:, d:2 * d]) * _mm(ys_ref[...], ws_ref[...])
              + jax.nn.sigmoid(gate_ref[:, 2 * d:3 * d]) * _mm(yr_ref[...], wr_ref[...]))
    mix = _mm(merged, wo_ref[...])
    o_ref[...] = _layer_norm(DN_ALPHA * x_ref[...] + mix, g_ref[...], b_ref[...])


def _merge(x2, h2, yg, ys, yr, wg, ws, wr, wo, g, b, tm=512):
    t = x2.shape[0]
    tm = min(tm, t)
    row = lambda w: pl.BlockSpec((tm, w), lambda i: (i, 0))
    const = lambda shape: pl.BlockSpec(shape, lambda i: (0,) * len(shape))
    return pl.pallas_call(
        _merge_kernel,
        grid=(t // tm,),
        in_specs=[row(D_MODEL), pl.BlockSpec((tm, COL_W), lambda i: (i, COL_GATE)),
                  row(D_MODEL), row(D_MODEL), row(D_MODEL),
                  const(wg.shape), const(ws.shape), const(wr.shape), const(wo.shape),
                  const(g.shape), const(b.shape)],
        out_specs=row(D_MODEL),
        out_shape=jax.ShapeDtypeStruct((t, D_MODEL), F32),
        compiler_params=_params(("parallel",)),
        name="merge",
    )(x2, h2, yg, ys, yr, wg, ws, wr, wo, g, b)


def _xattn_kernel(x_ref, kv_ref, wq_ref, wo_ref, g_ref, b_ref, o_ref):
    x = x_ref[...]
    q = _mm(x, wq_ref[...])
    outs = []
    for h in range(XA_HEADS):
        k = kv_ref[:, h * XA_DH:(h + 1) * XA_DH]
        v = kv_ref[:, D_MODEL + h * XA_DH:D_MODEL + (h + 1) * XA_DH]
        s = _mm_nt(q[:, h * XA_DH:(h + 1) * XA_DH], k) * XA_DH ** -0.5
        e = jnp.exp(s - jnp.max(s, -1, keepdims=True))
        outs.append(_mm(e / jnp.sum(e, -1, keepdims=True), v))
    xa = _mm(jnp.concatenate(outs, axis=-1), wo_ref[...])
    o_ref[...] = _layer_norm(DN_ALPHA * x + xa, g_ref[...], b_ref[...])


def _xattn(x3, kv3, wq, wo, g, b, tm=512):
    bsz, seq, _ = x3.shape
    tm = min(tm, seq)
    const = lambda shape: pl.BlockSpec(shape, lambda bb, i: (0,) * len(shape))
    return pl.pallas_call(
        _xattn_kernel,
        grid=(bsz, seq // tm),
        in_specs=[pl.BlockSpec((None, tm, D_MODEL), lambda bb, i: (bb, i, 0)),
                  pl.BlockSpec((None, MEM_LEN, 2 * D_MODEL), lambda bb, i: (bb, 0, 0)),
                  const(wq.shape), const(wo.shape), const(g.shape), const(b.shape)],
        out_specs=pl.BlockSpec((None, tm, D_MODEL), lambda bb, i: (bb, i, 0)),
        out_shape=jax.ShapeDtypeStruct((bsz, seq, D_MODEL), F32),
        compiler_params=_params(("parallel", "parallel")),
        name="xattn",
    )(x3, kv3, wq, wo, g, b)


def _top2_sum(a, b, c, d):
    return jnp.maximum(jnp.maximum(a, b) + jnp.maximum(c, d), jnp.maximum(a + b, c + d))


def _router_kernel(x_ref, wrt_ref, br_ref, e_ref, w_ref, rank_ref, cnt_ref, carry_ref):
    tm = x_ref.shape[0]
    epg = EXPERTS_PER_GROUP

    @pl.when(pl.program_id(0) == 0)
    def _():
        carry_ref[...] = jnp.zeros(carry_ref.shape, F32)

    scores = jax.nn.sigmoid(_mm_nt(wrt_ref[...], x_ref[...]))
    sel = scores + br_ref[...]
    rows = [sel[i:i + 1, :] for i in range(N_EXPERTS)]
    srow = [scores[i:i + 1, :] for i in range(N_EXPERTS)]
    gscore = [_top2_sum(*rows[g * epg:(g + 1) * epg]) for g in range(N_EXPERT_GROUPS)]
    best, gidx = gscore[0], jnp.zeros((1, tm), jnp.int32)
    for g in range(1, N_EXPERT_GROUPS):
        better = gscore[g] > best
        best = jnp.where(better, gscore[g], best)
        gidx = jnp.where(better, g, gidx)
    pick = lambda vals, j: functools.reduce(
        lambda acc, g: jnp.where(gidx == g, vals[g * epg + j], acc), range(1, N_EXPERT_GROUPS), vals[j])
    ing = [pick(rows, j) for j in range(epg)]
    ins = [pick(srow, j) for j in range(epg)]
    v0, l0, s0 = ing[0], jnp.zeros((1, tm), jnp.int32), ins[0]
    for j in range(1, epg):
        better = ing[j] > v0
        v0 = jnp.where(better, ing[j], v0)
        l0 = jnp.where(better, j, l0)
        s0 = jnp.where(better, ins[j], s0)
    v1 = jnp.full((1, tm), -jnp.inf, F32)
    l1 = jnp.zeros((1, tm), jnp.int32)
    s1 = jnp.zeros((1, tm), F32)
    for j in range(epg):
        better = (ing[j] > v1) & (l0 != j)
        v1 = jnp.where(better, ing[j], v1)
        l1 = jnp.where(better, j, l1)
        s1 = jnp.where(better, ins[j], s1)
    e0 = gidx * epg + l0
    e1 = gidx * epg + l1
    tot = s0 + s1
    e_ref[0:1, :] = e0
    e_ref[1:2, :] = e1
    w_ref[0:1, :] = s0 / tot
    w_ref[1:2, :] = s1 / tot
    eid = lax.broadcasted_iota(jnp.int32, (N_EXPERTS, tm), 0)
    oh0 = (eid == e0).astype(F32)
    oh1 = (eid == e1).astype(F32)
    ti = lax.broadcasted_iota(jnp.int32, (tm, tm), 0)
    tj = lax.broadcasted_iota(jnp.int32, (tm, tm), 1)
    before = _mm(oh0 + oh1, (ti < tj).astype(F32)) + carry_ref[...]
    rank_ref[0:1, :] = jnp.sum(oh0 * before, 0, keepdims=True).astype(jnp.int32)
    rank_ref[1:2, :] = jnp.sum(oh1 * before, 0, keepdims=True).astype(jnp.int32)
    carry_ref[...] = carry_ref[...] + jnp.sum(oh0 + oh1, 1, keepdims=True)
    cnt_ref[...] = carry_ref[...].astype(jnp.int32)


def _router(x2, wrt, br, tm=512):
    t = x2.shape[0]
    tm = min(tm, t)
    const = lambda shape: pl.BlockSpec(shape, lambda i: (0,) * len(shape))
    tok = pl.BlockSpec((TOP_K, tm), lambda i: (0, i))
    return pl.pallas_call(
        _router_kernel,
        grid=(t // tm,),
        in_specs=[pl.BlockSpec((tm, D_MODEL), lambda i: (i, 0)), const(wrt.shape), const(br.shape)],
        out_specs=[tok, tok, tok, const((N_EXPERTS, 1))],
        out_shape=[jax.ShapeDtypeStruct((TOP_K, t), jnp.int32), jax.ShapeDtypeStruct((TOP_K, t), F32),
                   jax.ShapeDtypeStruct((TOP_K, t), jnp.int32),
                   jax.ShapeDtypeStruct((N_EXPERTS, 1), jnp.int32)],
        scratch_shapes=[pltpu.VMEM((N_EXPERTS, 1), F32)],
        compiler_params=_params(("arbitrary",)),
        name="router",
    )(x2, wrt, br)


def _ffn_kernel(be_ref, nused_ref, x_ref, wgu_ref, wd_ref, o_ref):
    i = pl.program_id(0)

    @pl.when(i < nused_ref[0])
    def _():
        gu = _mm(x_ref[...], wgu_ref[...])
        hid = _silu(gu[:, 0:D_EXPERT]) * gu[:, D_EXPERT:2 * D_EXPERT]
        o_ref[...] = _mm(hid, wd_ref[...]).astype(o_ref.dtype)

    @pl.when(i >= nused_ref[0])
    def _():
        o_ref[...] = jnp.zeros(o_ref.shape, o_ref.dtype)


def _ffn(block_expert, n_used, xb, wgu, wd):
    npad = xb.shape[0]
    return pl.pallas_call(
        _ffn_kernel,
        grid_spec=pltpu.PrefetchScalarGridSpec(
            num_scalar_prefetch=2,
            grid=(npad // MOE_BLK,),
            in_specs=[pl.BlockSpec((MOE_BLK, D_MODEL), lambda i, be, nu: (i, 0)),
                      pl.BlockSpec((None, D_MODEL, 2 * D_EXPERT), lambda i, be, nu: (be[i], 0, 0)),
                      pl.BlockSpec((None, D_EXPERT, D_MODEL), lambda i, be, nu: (be[i], 0, 0))],
            out_specs=pl.BlockSpec((MOE_BLK, D_MODEL), lambda i, be, nu: (i, 0))),
        out_shape=jax.ShapeDtypeStruct((npad, D_MODEL), BF16),
        compiler_params=_params(("arbitrary",)),
        name="ffn",
    )(block_expert, n_used, xb, wgu, wd)


def _combine_kernel(x_ref, y0_ref, y1_ref, w_ref, g_ref, b_ref, o_ref):
    ff = y0_ref[...] * w_ref[:, 0:1] + y1_ref[...] * w_ref[:, 1:2]
    o_ref[...] = _layer_norm(DN_ALPHA * x_ref[...] + ff, g_ref[...], b_ref[...])


def _combine(x2, y0, y1, w, g, b, tm=512):
    t = x2.shape[0]
    tm = min(tm, t)
    row = lambda w_: pl.BlockSpec((tm, w_), lambda i: (i, 0))
    const = lambda shape: pl.BlockSpec(shape, lambda i: (0,) * len(shape))
    return pl.pallas_call(
        _combine_kernel,
        grid=(t // tm,),
        in_specs=[row(D_MODEL), row(D_MODEL), row(D_MODEL), row(TOP_K), const(g.shape), const(b.shape)],
        out_specs=row(D_MODEL),
        out_shape=jax.ShapeDtypeStruct((t, D_MODEL), F32),
        compiler_params=_params(("parallel",)),
        name="combine",
    )(x2, y0, y1, w, g, b)


def _moe(x2, wrt, br, wgu, wd, ln_g, ln_b):
    t = x2.shape[0]
    expert, gate, rank, counts = _router(x2, wrt, br)
    counts = counts[:, 0]
    padded = (counts + MOE_BLK - 1) // MOE_BLK * MOE_BLK
    pend = jnp.cumsum(padded)
    pstart = pend - padded
    n_blocks = (t * TOP_K + N_EXPERTS * (MOE_BLK - 1) + MOE_BLK - 1) // MOE_BLK
    npad = n_blocks * MOE_BLK
    dest = pstart[expert] + rank
    tok = jnp.broadcast_to(jnp.arange(t, dtype=jnp.int32)[None, :], (TOP_K, t))
    buf_tok = jnp.zeros((npad,), jnp.int32).at[dest.reshape(-1)].set(tok.reshape(-1))
    block_expert = jnp.minimum(
        jnp.searchsorted(pend, jnp.arange(n_blocks, dtype=jnp.int32) * MOE_BLK, side='right'),
        N_EXPERTS - 1).astype(jnp.int32)
    n_used = (pend[-1:] // MOE_BLK).astype(jnp.int32)
    xb = x2.astype(BF16)[buf_tok]
    yb = _ffn(block_expert, n_used, xb, wgu, wd)
    return _combine(x2, yb[dest[0]], yb[dest[1]], gate.T, ln_g, ln_b)


def _prep_weights(mix_w_in, gdn_conv_w, gdn_a_log, gdn_dt_bias, ssd_conv_w, ssd_conv_b, ssd_a_log, ssd_dt_bias,
                  ssd_d, moe_w_gate, moe_w_up, xa_w_k, xa_w_v, router_w, router_b):
    nl = mix_w_in.shape[0]
    off = [0]
    for s in (GDN_QKV, GDN_HEADS * GDN_DV, GDN_HEADS, GDN_HEADS, SSD_XBC, SSD_INNER, SSD_HEADS,
              RET_HEADS * RET_DK, RET_HEADS * RET_DK, RET_HEADS * RET_DV, RET_HEADS * RET_DV,
              N_BRANCH * D_MODEL):
        off.append(off[-1] + s)
    sl = lambda i, j=None: mix_w_in[:, :, off[i]:off[(i if j is None else j) + 1]]
    zeros = lambda n: jnp.zeros((nl, D_MODEL, n), mix_w_in.dtype)
    w_all = jnp.concatenate([
        sl(11),
        sl(0),
        sl(7, 10),
        sl(4, 5),
        sl(1),
        sl(2, 3), zeros(LANE - 2 * GDN_HEADS),
        sl(6), zeros(LANE - SSD_HEADS),
        zeros(PROJ_N - (COL_SSD_DT + 1) * LANE)], axis=-1).astype(BF16)
    wabt = jnp.swapaxes(sl(2, 3), 1, 2).astype(BF16)
    wdtt = jnp.swapaxes(sl(6), 1, 2).astype(BF16)

    def col_params(a_log, dt_bias, lane0):
        n = a_log.shape[1]
        p = jnp.zeros((nl, 2, LANE), F32)
        return p.at[:, 0, lane0:lane0 + n].set(a_log).at[:, 1, lane0:lane0 + n].set(dt_bias)

    def row_params(a_log, dt_bias):
        n = a_log.shape[1]
        p = jnp.zeros((nl, 16, 2), F32)
        return p.at[:, 0:n, 0].set(a_log).at[:, 0:n, 1].set(dt_bias)

    return dict(
        w_all=w_all, wabt=wabt, wdtt=wdtt,
        gdn_pcol=col_params(gdn_a_log, gdn_dt_bias, 0), gdn_prow=row_params(gdn_a_log, gdn_dt_bias),
        ssd_pcol=col_params(ssd_a_log, ssd_dt_bias, 0), ssd_prow=row_params(ssd_a_log, ssd_dt_bias),
        ssd_dskip=jnp.repeat(ssd_d, SSD_HEADDIM, axis=-1)[:, None, :],
        wgu=jnp.concatenate([moe_w_gate, moe_w_up], axis=-1).astype(BF16),
        wkv=jnp.concatenate([xa_w_k, xa_w_v], axis=-1).astype(BF16),
        conv_w=jnp.zeros((nl, CONV_K, PROJ_N), F32)
        .at[:, :, COL_GDN * COL_W:COL_GDN * COL_W + GDN_QKV].set(gdn_conv_w)
        .at[:, :, COL_SSD * COL_W:COL_SSD * COL_W + SSD_XBC].set(ssd_conv_w),
        conv_b=jnp.zeros((nl, 1, PROJ_N), F32)
        .at[:, 0, COL_SSD * COL_W:COL_SSD * COL_W + SSD_XBC].set(ssd_conv_b),
        wrt=router_w.T.astype(BF16), br=router_b[:, None],
    )


def kernel(x, mem, mix_w_in, gdn_conv_w, gdn_a_log, gdn_dt_bias, gdn_norm_w, ssd_conv_w, ssd_conv_b, ssd_a_log, ssd_dt_bias, ssd_d, ssd_norm_w, w_proj_gdn, w_proj_ssd, w_proj_ret, mix_w_out, ln_mix_g, ln_mix_b, xa_w_q, xa_w_k, xa_w_v, xa_w_o, ln_xa_g, ln_xa_b, router_w, router_b, moe_w_gate, moe_w_up, moe_w_down, ln_moe_g, ln_moe_b):
    bsz, seq, d = x.shape
    t = bsz * seq
    p = _prep_weights(mix_w_in, gdn_conv_w, gdn_a_log, gdn_dt_bias, ssd_conv_w, ssd_conv_b, ssd_a_log,
                      ssd_dt_bias, ssd_d, moe_w_gate, moe_w_up, xa_w_k, xa_w_v, router_w, router_b)
    bf = lambda w: w.astype(BF16)
    wpg, wps, wpr, wout = bf(w_proj_gdn), bf(w_proj_ssd), bf(w_proj_ret), bf(mix_w_out)
    wq, wo, wd = bf(xa_w_q), bf(xa_w_o), bf(moe_w_down)
    vec = lambda v, l: v[l][None, :]

    pos = jnp.arange(seq, dtype=F32)
    inv_freq = ROPE_BASE ** (-jnp.arange(0, RET_DK, 2, dtype=F32) / RET_DK)
    ang = pos[:, None] * inv_freq[None, :]
    cos, sin = jnp.cos(ang), jnp.sin(ang)
    cos2 = jnp.concatenate([cos, cos], axis=-1)
    sin2 = jnp.concatenate([-sin, sin], axis=-1)
    mem2 = mem.reshape(bsz * mem.shape[1], d)

    x2 = x.reshape(t, d)
    for l in range(DEPTH):
        h2 = _proj(x2, p["w_all"][l], p["conv_w"][l], p["conv_b"][l], seq)
        x3, h3 = x2.reshape(bsz, seq, d), h2.reshape(bsz, seq, PROJ_N)
        yg = _gdn(x3, h3, p["wabt"][l], p["gdn_pcol"][l], p["gdn_prow"][l], vec(gdn_norm_w, l))
        ys = _ssd(x3, h3, p["wdtt"][l], p["ssd_pcol"][l], p["ssd_prow"][l], p["ssd_dskip"][l],
                  vec(ssd_norm_w, l))
        yr = _ret(h3, cos2, sin2)
        x2 = _merge(x2, h2, yg.reshape(t, d), ys.reshape(t, d), yr.reshape(t, d), wpg[l], wps[l], wpr[l],
                    wout[l], vec(ln_mix_g, l), vec(ln_mix_b, l))
        kv = _matmul(mem2, p["wkv"][l], BF16, 512, 1024)
        x2 = _xattn(x2.reshape(bsz, seq, d), kv.reshape(bsz, mem.shape[1], 2 * d), wq[l], wo[l],
                    vec(ln_xa_g, l), vec(ln_xa_b, l)).reshape(t, d)
        x2 = _moe(x2, p["wrt"], p["br"], p["wgu"][l], wd[l], vec(ln_moe_g, l), vec(ln_moe_b, l))
    return x2.reshape(bsz, seq, d)
```

```python
import functools
import math

import jax
import jax.numpy as jnp
from jax import lax
from jax.experimental import pallas as pl
from jax.experimental.pallas import tpu as pltpu

F32 = jnp.float32
BF16 = jnp.bfloat16

D_MODEL = 1024
DEPTH = 4
CHUNK = 64
CONV_K = 4
GDN_HEADS = 8
GDN_DK = 128
GDN_DV = 128
SSD_INNER = D_MODEL
SSD_HEADDIM = 64
SSD_HEADS = SSD_INNER // SSD_HEADDIM
SSD_GROUPS = 4
SSD_HPG = SSD_HEADS // SSD_GROUPS
SSD_STATE = 128
RET_HEADS = 4
RET_DK = 128
RET_DV = 256
ROPE_BASE = 10000.0
N_BRANCH = 3
MEM_LEN = 256
XA_HEADS = 4
XA_DH = D_MODEL // XA_HEADS
N_EXPERTS = 16
N_EXPERT_GROUPS = 4
EXPERTS_PER_GROUP = N_EXPERTS // N_EXPERT_GROUPS
TOP_K = 2
D_EXPERT = 512
DN_ALPHA = (2 * DEPTH) ** 0.25
LN_EPS = 1e-5
RMS_EPS = 1e-6

GDN_QKV = 2 * GDN_HEADS * GDN_DK + GDN_HEADS * GDN_DV
SSD_XBC = SSD_INNER + 2 * SSD_GROUPS * SSD_STATE

LANE = 128
COL_W = 3072
COL_GATE = 0
COL_GDN = 1
COL_RET = 2
COL_SSD = 3
COL_GDN_Z = 12
COL_GDN_AB = 104
COL_SSD_DT = 105
PROJ_N = 13824
PROJ_TM = 1024
PROJ_TN = 1536
PROJ_HALO = 16
PROJ_CONV_TILES = (COL_GDN * COL_W // PROJ_TN, COL_GDN * COL_W // PROJ_TN + 1, COL_SSD * COL_W // PROJ_TN)
PROJ_CONV_PART_TILE = COL_SSD * COL_W // PROJ_TN + 1
PROJ_CONV_PART_COLS = SSD_XBC - PROJ_TN
PROJ_VMEM_LIMIT = 56 * 1024 * 1024
GDN_TILE = 256
SSD_TILE = 256
RET_TILE = 256

MOE_BLK = 512
MOE_TM = 512
VMEM_LIMIT = 48 * 1024 * 1024


def _mm(a, b):
    return jnp.dot(a.astype(BF16), b.astype(BF16), preferred_element_type=F32)


def _mm_nt(a, b):
    return lax.dot_general(a.astype(BF16), b.astype(BF16), (((1,), (1,)), ((), ())),
                           preferred_element_type=F32)


def _mm_tn(a, b):
    return lax.dot_general(a.astype(BF16), b.astype(BF16), (((0,), (0,)), ((), ())),
                           preferred_element_type=F32)


def _mm_f32(a, b):
    return jnp.dot(a, b, preferred_element_type=F32, precision=lax.Precision.HIGHEST)


def _silu(x):
    hx = 0.5 * x
    return hx + hx * jnp.tanh(hx)


def _softplus(x):
    return jnp.maximum(x, 0.0) + jnp.log1p(jnp.exp(-jnp.abs(x)))


def _layer_norm(v, g, b):
    mu = jnp.mean(v, -1, keepdims=True)
    var = jnp.mean(jnp.square(v - mu), -1, keepdims=True)
    return (v - mu) * lax.rsqrt(var + LN_EPS) * g + b


def _params(sem):
    return pltpu.CompilerParams(dimension_semantics=sem, vmem_limit_bytes=VMEM_LIMIT)


def _matmul_kernel(x_ref, w_ref, o_ref):
    o_ref[...] = _mm(x_ref[...], w_ref[...]).astype(o_ref.dtype)


def _matmul(x, w, out_dtype, tm, tn):
    m, k = x.shape
    n = w.shape[1]
    tm, tn = min(tm, m), min(tn, n)
    return pl.pallas_call(
        _matmul_kernel,
        grid=(m // tm, n // tn),
        in_specs=[pl.BlockSpec((tm, k), lambda i, j: (i, 0)),
                  pl.BlockSpec((k, tn), lambda i, j: (0, j))],
        out_specs=pl.BlockSpec((tm, tn), lambda i, j: (i, j)),
        out_shape=jax.ShapeDtypeStruct((m, n), out_dtype),
        compiler_params=_params(("parallel", "arbitrary")),
        name="matmul",
    )(x, w)


def _proj_kernel(x_ref, xprev_ref, w_ref, cw_ref, cb_ref, o_ref, xs_ref, res_ref, *, tiles_per_seq):
    i, j = pl.program_id(0), pl.program_id(1)
    tm, tn = o_ref.shape

    @pl.when(j == 0)
    def _():
        xs_ref[PROJ_HALO:PROJ_HALO + tm, :] = x_ref[...].astype(BF16)
        starts_sequence = (i % tiles_per_seq) == 0
        xs_ref[0:PROJ_HALO, :] = jnp.where(starts_sequence, 0.0, xprev_ref[...]).astype(BF16)

    r = jnp.dot(xs_ref[...], w_ref[...], preferred_element_type=F32)

    def conv_silu(ncols):
        res_ref[:, 0:ncols] = r[:, 0:ncols]
        acc = r[PROJ_HALO:, 0:ncols] * cw_ref[CONV_K - 1:CONV_K, 0:ncols] + cb_ref[:, 0:ncols]
        for tap in range(CONV_K - 1):
            off = PROJ_HALO - (CONV_K - 1) + tap
            acc = acc + res_ref[off:off + tm, 0:ncols] * cw_ref[tap:tap + 1, 0:ncols]
        return _silu(acc)

    conv_full = (j == PROJ_CONV_TILES[0]) | (j == PROJ_CONV_TILES[1]) | (j == PROJ_CONV_TILES[2])
    conv_part = j == PROJ_CONV_PART_TILE

    @pl.when(conv_full)
    def _():
        o_ref[...] = conv_silu(tn)

    @pl.when(conv_part)
    def _():
        o_ref[:, 0:PROJ_CONV_PART_COLS] = conv_silu(PROJ_CONV_PART_COLS)
        o_ref[:, PROJ_CONV_PART_COLS:] = r[PROJ_HALO:, PROJ_CONV_PART_COLS:]

    @pl.when(jnp.logical_not(conv_full | conv_part))
    def _():
        o_ref[...] = r[PROJ_HALO:, :]


def _proj(x2, w, cw, cb, seq):
    t, d = x2.shape
    tm = min(PROJ_TM, seq)
    tn = PROJ_TN
    halo_blocks = tm // PROJ_HALO
    return pl.pallas_call(
        functools.partial(_proj_kernel, tiles_per_seq=seq // tm),
        grid=(t // tm, PROJ_N // tn),
        in_specs=[pl.BlockSpec((tm, d), lambda i, j: (i, 0)),
                  pl.BlockSpec((PROJ_HALO, d), lambda i, j: (jnp.maximum(i * halo_blocks - 1, 0), 0)),
                  pl.BlockSpec((d, tn), lambda i, j: (0, j)),
                  pl.BlockSpec((CONV_K, tn), lambda i, j: (0, j)),
                  pl.BlockSpec((1, tn), lambda i, j: (0, j))],
        out_specs=pl.BlockSpec((tm, tn), lambda i, j: (i, j)),
        out_shape=jax.ShapeDtypeStruct((t, PROJ_N), F32),
        scratch_shapes=[pltpu.VMEM((PROJ_HALO + tm, d), BF16),
                        pltpu.VMEM((PROJ_HALO + tm, tn), F32)],
        compiler_params=pltpu.CompilerParams(dimension_semantics=("parallel", "arbitrary"),
                                             vmem_limit_bytes=PROJ_VMEM_LIMIT),
        name="proj",
    )(x2, x2, w, cw, cb)


def _tri_masks(n):
    row = lax.broadcasted_iota(jnp.int32, (n, n), 0)
    col = lax.broadcasted_iota(jnp.int32, (n, n), 1)
    return row >= col, row > col


def _upper_ones(n):
    row = lax.broadcasted_iota(jnp.int32, (n, n), 0)
    col = lax.broadcasted_iota(jnp.int32, (n, n), 1)
    return (row <= col).astype(F32)


def _decay_matrix(cum_c, cum_r, causal):
    return jnp.exp(jnp.where(causal, cum_c - cum_r, -jnp.inf))


def _chunk_rows(c):
    return slice(c * CHUNK, (c + 1) * CHUNK)


def _chunk_cumsums(la_col, la_row, nch):
    tri = _tri_masks(CHUNK)[0].astype(F32)
    upper = _upper_ones(CHUNK)
    cum_col = [_mm_f32(tri, la_col[_chunk_rows(c), :]) for c in range(nch)]
    cum_row = [_mm_f32(la_row[:, _chunk_rows(c)], upper) for c in range(nch)]
    return cum_col, cum_row


def _gdn_kernel(x_ref, qkv_ref, z_ref, ab_ref, wabt_ref, pcol_ref, prow_ref, normw_ref, o_ref, state_ref):
    L = qkv_ref.shape[0]
    C = CHUNK
    nch = L // C
    H = range(GDN_HEADS)

    @pl.when(pl.program_id(1) == 0)
    def _():
        state_ref[...] = jnp.zeros(state_ref.shape, F32)

    causal, strict = _tri_masks(C)
    eye = (causal & ~strict).astype(F32)

    ab = ab_ref[...]
    la_col = -jnp.exp(pcol_ref[0:1, :]) * _softplus(ab + pcol_ref[1:2, :])
    beta_col = jax.nn.sigmoid(ab)
    ab_t = _mm_nt(wabt_ref[...], x_ref[...])
    la_row = -jnp.exp(prow_ref[:, 0:1]) * _softplus(ab_t + prow_ref[:, 1:2])
    cum_col, cum_row = _chunk_cumsums(la_col, la_row, nch)

    P = [(c, h) for c in range(nch) for h in H]
    I = range(len(P))
    k_off, v_off = GDN_HEADS * GDN_DK, 2 * GDN_HEADS * GDN_DK
    qs = [qkv_ref[_chunk_rows(c), h * GDN_DK:(h + 1) * GDN_DK] for c, h in P]
    ks = [qkv_ref[_chunk_rows(c), k_off + h * GDN_DK:k_off + (h + 1) * GDN_DK] for c, h in P]
    vs = [qkv_ref[_chunk_rows(c), v_off + h * GDN_DV:v_off + (h + 1) * GDN_DV] for c, h in P]
    qs = [q * (lax.rsqrt(jnp.sum(jnp.square(q), -1, keepdims=True) + RMS_EPS) * GDN_DK ** -0.5) for q in qs]
    ks = [k * lax.rsqrt(jnp.sum(jnp.square(k), -1, keepdims=True) + RMS_EPS) for k in ks]
    cum_c = [cum_col[c][:, h:h + 1] for c, h in P]
    beta = [beta_col[_chunk_rows(c), GDN_HEADS + h:GDN_HEADS + h + 1] for c, h in P]
    decay = [_decay_matrix(cum_c[i], cum_row[c][h:h + 1, :], causal) for i, (c, h) in enumerate(P)]
    kb = [ks[i] * beta[i] for i in I]
    vb = [vs[i] * beta[i] for i in I]
    kq_k = [_mm_nt(jnp.concatenate([kb[i], qs[i]], axis=0), ks[i]) for i in I]
    n_pow = [-jnp.where(strict, kq_k[i][0:C] * decay[i], 0.0) for i in I]
    attn = [jnp.where(causal, kq_k[i][C:2 * C] * decay[i], 0.0) for i in I]
    t_mat = [eye + n_pow[i] for i in I]
    n_pow = [_mm(n_pow[i], n_pow[i]) for i in I]
    for _ in range(int(math.log2(C)) - 2):
        both = [_mm(jnp.concatenate([t_mat[i], n_pow[i]], axis=0), n_pow[i]) for i in I]
        t_mat = [t_mat[i] + both[i][0:C] for i in I]
        n_pow = [both[i][C:2 * C] for i in I]
    upd = [_mm(t_mat[i], n_pow[i]) for i in I]
    t_mat = [t_mat[i] + upd[i] for i in I]
    e_cum = [jnp.exp(cum_c[i]) for i in I]
    uw = [_mm(t_mat[i], jnp.concatenate([vb[i], kb[i] * e_cum[i]], axis=-1)) for i in I]
    u = [uw[i][:, 0:GDN_DV] for i in I]
    w = [uw[i][:, GDN_DV:] for i in I]
    last = [cum_c[i][C - 1:C, :] for i in I]
    wq = [jnp.concatenate([w[i], qs[i] * e_cum[i]], axis=0) for i in I]
    ak = [jnp.concatenate([attn[i], (ks[i] * jnp.exp(last[i] - cum_c[i])).T], axis=0) for i in I]

    states = [state_ref[h] for h in H]
    outs = []
    for c in range(nch):
        ix = [c * GDN_HEADS + h for h in H]
        wq_s = [_mm(wq[i], states[h]) for h, i in zip(H, ix)]
        v_new = [u[i] - wq_s[h][0:C] for h, i in zip(H, ix)]
        ak_v = [_mm(ak[i], v_new[h]) for h, i in zip(H, ix)]
        states = [states[h] * jnp.exp(last[i]) + ak_v[h][C:] for h, i in zip(H, ix)]
        outs += [wq_s[h][C:2 * C] + ak_v[h][0:C] for h in H]
    for h in H:
        state_ref[h] = states[h]
    scale = [lax.rsqrt(jnp.mean(jnp.square(o), -1, keepdims=True) + RMS_EPS) for o in outs]
    gate = [_silu(z_ref[_chunk_rows(c), h * GDN_DV:(h + 1) * GDN_DV]) for c, h in P]
    for i, (c, h) in enumerate(P):
        o_ref[_chunk_rows(c), h * GDN_DV:(h + 1) * GDN_DV] = (
            outs[i] * scale[i] * normw_ref[...] * gate[i]).astype(o_ref.dtype)


def _gdn(x3, h3, wabt, pcol, prow, normw):
    bsz, seq, _ = x3.shape
    L = min(GDN_TILE, seq)
    const = lambda shape: pl.BlockSpec(shape, lambda b, c: (0,) * len(shape))
    return pl.pallas_call(
        _gdn_kernel,
        grid=(bsz, seq // L),
        in_specs=[pl.BlockSpec((None, L, D_MODEL), lambda b, c: (b, c, 0)),
                  pl.BlockSpec((None, L, COL_W), lambda b, c: (b, c, COL_GDN)),
                  pl.BlockSpec((None, L, D_MODEL), lambda b, c: (b, c, COL_GDN_Z)),
                  pl.BlockSpec((None, L, LANE), lambda b, c: (b, c, COL_GDN_AB)),
                  const(wabt.shape), const(pcol.shape), const(prow.shape), const(normw.shape)],
        out_specs=pl.BlockSpec((None, L, D_MODEL), lambda b, c: (b, c, 0)),
        out_shape=jax.ShapeDtypeStruct((bsz, seq, GDN_HEADS * GDN_DV), BF16),
        scratch_shapes=[pltpu.VMEM((GDN_HEADS, GDN_DK, GDN_DV), F32)],
        compiler_params=_params(("parallel", "arbitrary")),
        name="gdn",
    )(x3, h3, h3, h3, wabt, pcol, prow, normw)


def _ssd_kernel(x_ref, xbcz_ref, dt_ref, wdtt_ref, pcol_ref, prow_ref, dskip_ref, normw_ref, o_ref, state_ref):
    L = xbcz_ref.shape[0]
    C = CHUNK
    nch = L // C
    G = range(SSD_GROUPS)
    R = range(SSD_HPG)
    gw = SSD_HPG * SSD_HEADDIM
    b_off, c_off = SSD_INNER, SSD_INNER + SSD_GROUPS * SSD_STATE

    @pl.when(pl.program_id(1) == 0)
    def _():
        state_ref[...] = jnp.zeros(state_ref.shape, F32)

    causal, _ = _tri_masks(C)
    dt_col = _softplus(dt_ref[...] + pcol_ref[1:2, :])
    la_col = -jnp.exp(pcol_ref[0:1, :]) * dt_col
    dt_t = _mm_nt(wdtt_ref[...], x_ref[...])
    la_row = -jnp.exp(prow_ref[:, 0:1]) * _softplus(dt_t + prow_ref[:, 1:2])
    cum_col, cum_row = _chunk_cumsums(la_col, la_row, nch)

    PG = [(c, g) for c in range(nch) for g in G]
    bm = {(c, g): xbcz_ref[_chunk_rows(c), b_off + g * SSD_STATE:b_off + (g + 1) * SSD_STATE] for c, g in PG}
    cm = {(c, g): xbcz_ref[_chunk_rows(c), c_off + g * SSD_STATE:c_off + (g + 1) * SSD_STATE] for c, g in PG}
    scores = {p: _mm_nt(cm[p], bm[p]) for p in PG}
    PH = [(c, g, r) for c in range(nch) for g in G for r in R]
    head = lambda g, r: g * SSD_HPG + r
    xs = {(c, g, r): xbcz_ref[_chunk_rows(c), head(g, r) * SSD_HEADDIM:(head(g, r) + 1) * SSD_HEADDIM]
          for c, g, r in PH}
    cum_c = {(c, g, r): cum_col[c][:, head(g, r):head(g, r) + 1] for c, g, r in PH}
    decay = {(c, g, r): _decay_matrix(cum_c[c, g, r], cum_row[c][head(g, r):head(g, r) + 1, :], causal)
             for c, g, r in PH}
    v = {(c, g, r): xs[c, g, r] * dt_col[_chunk_rows(c), head(g, r):head(g, r) + 1] for c, g, r in PH}
    y_intra = {(c, g, r): _mm(scores[c, g] * decay[c, g, r], v[c, g, r]) for c, g, r in PH}
    last = {p: cum_c[p][C - 1:C, :] for p in PH}
    v_k = {p: v[p] * jnp.exp(last[p] - cum_c[p]) for p in PH}
    k_v = {(c, g): _mm_tn(bm[c, g], jnp.concatenate([v_k[c, g, r] for r in R], axis=-1)) for c, g in PG}
    lane_head = lax.broadcasted_iota(jnp.int32, (1, gw), 1) // SSD_HEADDIM
    spread = lambda vals: functools.reduce(
        lambda acc, r: jnp.where(lane_head == r, vals[r], acc), R[1:],
        jnp.broadcast_to(vals[0], (vals[0].shape[0], gw)))

    states = [state_ref[g] for g in G]
    y_state = {}
    for c in range(nch):
        for g in G:
            y_state[c, g] = _mm(cm[c, g], states[g])
        states = [states[g] * spread([jnp.exp(last[c, g, r]) for r in R]) + k_v[c, g] for g in G]
    for g in G:
        state_ref[g] = states[g]
    e_cum = {(c, g): spread([jnp.exp(cum_c[c, g, r]) for r in R]) for c, g in PG}
    cols = lambda g: slice(g * gw, (g + 1) * gw)
    y = {(c, g): (jnp.concatenate([y_intra[c, g, r] for r in R], axis=-1) + y_state[c, g] * e_cum[c, g]
                  + dskip_ref[:, cols(g)] * xbcz_ref[_chunk_rows(c), cols(g)]) for c, g in PG}
    y = {(c, g): y[c, g] * _silu(xbcz_ref[_chunk_rows(c), SSD_XBC + g * gw:SSD_XBC + (g + 1) * gw]) for c, g in PG}
    scale = {p: lax.rsqrt(jnp.mean(jnp.square(y[p]), -1, keepdims=True) + RMS_EPS) for p in PG}
    for c, g in PG:
        o_ref[_chunk_rows(c), cols(g)] = (y[c, g] * scale[c, g] * normw_ref[:, cols(g)]).astype(o_ref.dtype)


def _ssd(x3, h3, wdtt, pcol, prow, dskip, normw):
    bsz, seq, _ = x3.shape
    L = min(SSD_TILE, seq)
    const = lambda shape: pl.BlockSpec(shape, lambda b, c: (0,) * len(shape))
    return pl.pallas_call(
        _ssd_kernel,
        grid=(bsz, seq // L),
        in_specs=[pl.BlockSpec((None, L, D_MODEL), lambda b, c: (b, c, 0)),
                  pl.BlockSpec((None, L, COL_W), lambda b, c: (b, c, COL_SSD)),
                  pl.BlockSpec((None, L, LANE), lambda b, c: (b, c, COL_SSD_DT)),
                  const(wdtt.shape), const(pcol.shape), const(prow.shape), const(dskip.shape),
                  const(normw.shape)],
        out_specs=pl.BlockSpec((None, L, D_MODEL), lambda b, c: (b, c, 0)),
        out_shape=jax.ShapeDtypeStruct((bsz, seq, SSD_INNER), BF16),
        scratch_shapes=[pltpu.VMEM((SSD_GROUPS, SSD_STATE, SSD_HPG * SSD_HEADDIM), F32)],
        compiler_params=_params(("parallel", "arbitrary")),
        name="ssd",
    )(x3, h3, h3, wdtt, pcol, prow, dskip, normw)


def _ret_kernel(qkvg_ref, cos_ref, sin_ref, o_ref, state_ref):
    L = qkvg_ref.shape[0]
    C = CHUNK
    nch = L // C
    H = range(RET_HEADS)

    @pl.when(pl.program_id(1) == 0)
    def _():
        state_ref[...] = jnp.zeros(state_ref.shape, F32)

    causal, _ = _tri_masks(C)
    row = lax.broadcasted_iota(jnp.int32, (C, C), 0)
    col = lax.broadcasted_iota(jnp.int32, (C, C), 1)
    dist = (row - col).astype(F32)
    pos = lax.broadcasted_iota(jnp.int32, (C, 1), 0).astype(F32)
    log_gamma = [math.log1p(-2.0 ** (-5.0 - h)) for h in H]
    decay = [jnp.exp(jnp.where(causal, dist * log_gamma[h], -jnp.inf)) for h in H]
    e_cum = [jnp.exp((pos + 1.0) * log_gamma[h]) for h in H]
    e_rest = [jnp.exp((C - 1.0 - pos) * log_gamma[h]) for h in H]
    k_off = RET_HEADS * RET_DK
    v_off = 2 * RET_HEADS * RET_DK
    g_off = v_off + RET_HEADS * RET_DV

    P = [(c, h) for c in range(nch) for h in H]
    cos2 = {c: cos_ref[_chunk_rows(c), :] for c in range(nch)}
    sin2 = {c: sin_ref[_chunk_rows(c), :] for c in range(nch)}
    rot = lambda t, c: t * cos2[c] + pltpu.roll(t, RET_DK // 2, 1) * sin2[c]
    q = {(c, h): rot(qkvg_ref[_chunk_rows(c), h * RET_DK:(h + 1) * RET_DK], c) for c, h in P}
    k = {(c, h): rot(qkvg_ref[_chunk_rows(c), k_off + h * RET_DK:k_off + (h + 1) * RET_DK], c) * RET_DK ** -0.5
         for c, h in P}
    v = {(c, h): qkvg_ref[_chunk_rows(c), v_off + h * RET_DV:v_off + (h + 1) * RET_DV] for c, h in P}
    scores = {p: _mm_nt(q[p], k[p]) for p in P}
    y_intra = {(c, h): _mm(scores[c, h] * decay[h], v[c, h]) for c, h in P}
    k_v = {(c, h): _mm_tn(k[c, h], v[c, h] * e_rest[h]) for c, h in P}

    states = [state_ref[h] for h in H]
    for c in range(nch):
        y_state = [_mm(q[c, h], states[h]) for h in H]
        states = [states[h] * math.exp(C * log_gamma[h]) + k_v[c, h] for h in H]
        for h in H:
            o = y_intra[c, h] + y_state[h] * e_cum[h]
            mu = jnp.mean(o, -1, keepdims=True)
            var = jnp.mean(jnp.square(o - mu), -1, keepdims=True)
            o = (o - mu) * lax.rsqrt(var + LN_EPS)
            o = _silu(qkvg_ref[_chunk_rows(c), g_off + h * RET_DV:g_off + (h + 1) * RET_DV]) * o
            o_ref[_chunk_rows(c), h * RET_DV:(h + 1) * RET_DV] = o.astype(o_ref.dtype)
    for h in H:
        state_ref[h] = states[h]


def _ret(h3, cos2, sin2):
    bsz, seq, _ = h3.shape
    L = min(RET_TILE, seq)
    return pl.pallas_call(
        _ret_kernel,
        grid=(bsz, seq // L),
        in_specs=[pl.BlockSpec((None, L, COL_W), lambda b, c: (b, c, COL_RET)),
                  pl.BlockSpec((L, RET_DK), lambda b, c: (c, 0)),
                  pl.BlockSpec((L, RET_DK), lambda b, c: (c, 0))],
        out_specs=pl.BlockSpec((None, L, D_MODEL), lambda b, c: (b, c, 0)),
        out_shape=jax.ShapeDtypeStruct((bsz, seq, RET_HEADS * RET_DV), BF16),
        scratch_shapes=[pltpu.VMEM((RET_HEADS, RET_DK, RET_DV), F32)],
        compiler_params=_params(("parallel", "arbitrary")),
        name="ret",
    )(h3, cos2, sin2)


def _merge_kernel(x_ref, gate_ref, yg_ref, ys_ref, yr_ref, wg_ref, ws_ref, wr_ref, wo_ref, g_ref, b_ref,
                  o_ref):
    d = D_MODEL
    merged = (jax.nn.sigmoid(gate_ref[:, 0:d]) * _mm(yg_ref[...], wg_ref[...])
              + jax.nn.sigmoid(gate_ref[:, d:2 * d]) * _mm(ys_ref[...], ws_ref[...])
              + jax.nn.sigmoid(gate_ref[:, 2 * d:3 * d]) * _mm(yr_ref[...], wr_ref[...]))
    mix = _mm(merged, wo_ref[...])
    o_ref[...] = _layer_norm(DN_ALPHA * x_ref[...] + mix, g_ref[...], b_ref[...])


def _merge(x2, h2, yg, ys, yr, wg, ws, wr, wo, g, b, tm=512):
    t = x2.shape[0]
    tm = min(tm, t)
    row = lambda w: pl.BlockSpec((tm, w), lambda i: (i, 0))
    const = lambda shape: pl.BlockSpec(shape, lambda i: (0,) * len(shape))
    return pl.pallas_call(
        _merge_kernel,
        grid=(t // tm,),
        in_specs=[row(D_MODEL), pl.BlockSpec((tm, COL_W), lambda i: (i, COL_GATE)),
                  row(D_MODEL), row(D_MODEL), row(D_MODEL),
                  const(wg.shape), const(ws.shape), const(wr.shape), const(wo.shape),
                  const(g.shape), const(b.shape)],
        out_specs=row(D_MODEL),
        out_shape=jax.ShapeDtypeStruct((t, D_MODEL), F32),
        compiler_params=_params(("parallel",)),
        name="merge",
    )(x2, h2, yg, ys, yr, wg, ws, wr, wo, g, b)


def _xattn_kernel(x_ref, kv_ref, wq_ref, wo_ref, g_ref, b_ref, o_ref):
    x = x_ref[...]
    q = _mm(x, wq_ref[...])
    outs = []
    for h in range(XA_HEADS):
        k = kv_ref[:, h * XA_DH:(h + 1) * XA_DH]
        v = kv_ref[:, D_MODEL + h * XA_DH:D_MODEL + (h + 1) * XA_DH]
        s = _mm_nt(q[:, h * XA_DH:(h + 1) * XA_DH], k) * XA_DH ** -0.5
        e = jnp.exp(s - jnp.max(s, -1, keepdims=True))
        outs.append(_mm(e / jnp.sum(e, -1, keepdims=True), v))
    xa = _mm(jnp.concatenate(outs, axis=-1), wo_ref[...])
    o_ref[...] = _layer_norm(DN_ALPHA * x + xa, g_ref[...], b_ref[...])


def _xattn(x3, kv3, wq, wo, g, b, tm=512):
    bsz, seq, _ = x3.shape
    tm = min(tm, seq)
    const = lambda shape: pl.BlockSpec(shape, lambda bb, i: (0,) * len(shape))
    return pl.pallas_call(
        _xattn_kernel,
        grid=(bsz, seq // tm),
        in_specs=[pl.BlockSpec((None, tm, D_MODEL), lambda bb, i: (bb, i, 0)),
                  pl.BlockSpec((None, MEM_LEN, 2 * D_MODEL), lambda bb, i: (bb, 0, 0)),
                  const(wq.shape), const(wo.shape), const(g.shape), const(b.shape)],
        out_specs=pl.BlockSpec((None, tm, D_MODEL), lambda bb, i: (bb, i, 0)),
        out_shape=jax.ShapeDtypeStruct((bsz, seq, D_MODEL), F32),
        compiler_params=_params(("parallel", "parallel")),
        name="xattn",
    )(x3, kv3, wq, wo, g, b)


def _top2_sum(a, b, c, d):
    return jnp.maximum(jnp.maximum(a, b) + jnp.maximum(c, d), jnp.maximum(a + b, c + d))


def _router_kernel(x_ref, wrt_ref, br_ref, e_ref, w_ref, rank_ref, cnt_ref, carry_ref):
    tm = x_ref.shape[0]
    epg = EXPERTS_PER_GROUP

    @pl.when(pl.program_id(0) == 0)
    def _():
        carry_ref[...] = jnp.zeros(carry_ref.shape, F32)

    scores = jax.nn.sigmoid(_mm_nt(wrt_ref[...], x_ref[...]))
    sel = scores + br_ref[...]
    rows = [sel[i:i + 1, :] for i in range(N_EXPERTS)]
    srow = [scores[i:i + 1, :] for i in range(N_EXPERTS)]
    gscore = [_top2_sum(*rows[g * epg:(g + 1) * epg]) for g in range(N_EXPERT_GROUPS)]
    best, gidx = gscore[0], jnp.zeros((1, tm), jnp.int32)
    for g in range(1, N_EXPERT_GROUPS):
        better = gscore[g] > best
        best = jnp.where(better, gscore[g], best)
        gidx = jnp.where(better, g, gidx)
    pick = lambda vals, j: functools.reduce(
        lambda acc, g: jnp.where(gidx == g, vals[g * epg + j], acc), range(1, N_EXPERT_GROUPS), vals[j])
    ing = [pick(rows, j) for j in range(epg)]
    ins = [pick(srow, j) for j in range(epg)]
    v0, l0, s0 = ing[0], jnp.zeros((1, tm), jnp.int32), ins[0]
    for j in range(1, epg):
        better = ing[j] > v0
        v0 = jnp.where(better, ing[j], v0)
        l0 = jnp.where(better, j, l0)
        s0 = jnp.where(better, ins[j], s0)
    v1 = jnp.full((1, tm), -jnp.inf, F32)
    l1 = jnp.zeros((1, tm), jnp.int32)
    s1 = jnp.zeros((1, tm), F32)
    for j in range(epg):
        better = (ing[j] > v1) & (l0 != j)
        v1 = jnp.where(better, ing[j], v1)
        l1 = jnp.where(better, j, l1)
        s1 = jnp.where(better, ins[j], s1)
    e0 = gidx * epg + l0
    e1 = gidx * epg + l1
    tot = s0 + s1
    e_ref[0:1, :] = e0
    e_ref[1:2, :] = e1
    w_ref[0:1, :] = s0 / tot
    w_ref[1:2, :] = s1 / tot
    eid = lax.broadcasted_iota(jnp.int32, (N_EXPERTS, tm), 0)
    oh0 = (eid == e0).astype(F32)
    oh1 = (eid == e1).astype(F32)
    ti = lax.broadcasted_iota(jnp.int32, (tm, tm), 0)
    tj = lax.broadcasted_iota(jnp.int32, (tm, tm), 1)
    before = _mm(oh0 + oh1, (ti < tj).astype(F32)) + carry_ref[...]
    rank_ref[0:1, :] = jnp.sum(oh0 * before, 0, keepdims=True).astype(jnp.int32)
    rank_ref[1:2, :] = jnp.sum(oh1 * before, 0, keepdims=True).astype(jnp.int32)
    carry_ref[...] = carry_ref[...] + jnp.sum(oh0 + oh1, 1, keepdims=True)
    cnt_ref[...] = carry_ref[...].astype(jnp.int32)


def _router(x2, wrt, br, tm=512):
    t = x2.shape[0]
    tm = min(tm, t)
    const = lambda shape: pl.BlockSpec(shape, lambda i: (0,) * len(shape))
    tok = pl.BlockSpec((TOP_K, tm), lambda i: (0, i))
    return pl.pallas_call(
        _router_kernel,
        grid=(t // tm,),
        in_specs=[pl.BlockSpec((tm, D_MODEL), lambda i: (i, 0)), const(wrt.shape), const(br.shape)],
        out_specs=[tok, tok, tok, const((N_EXPERTS, 1))],
        out_shape=[jax.ShapeDtypeStruct((TOP_K, t), jnp.int32), jax.ShapeDtypeStruct((TOP_K, t), F32),
                   jax.ShapeDtypeStruct((TOP_K, t), jnp.int32),
                   jax.ShapeDtypeStruct((N_EXPERTS, 1), jnp.int32)],
        scratch_shapes=[pltpu.VMEM((N_EXPERTS, 1), F32)],
        compiler_params=_params(("arbitrary",)),
        name="router",
    )(x2, wrt, br)


def _dispatch_kernel(dest_ref, x_hbm, init_hbm, xs_hbm, sem):
    del init_hbm
    tm = dest_ref.shape[1]
    base = pl.program_id(0) * tm

    def row_copy(t, k):
        return pltpu.make_async_copy(x_hbm.at[pl.ds(base + t, 1)], xs_hbm.at[pl.ds(dest_ref[k, t], 1)], sem)

    @pl.loop(0, tm)
    def _(t):
        for k in range(TOP_K):
            row_copy(t, k).start()

    @pl.loop(0, tm)
    def _(t):
        for k in range(TOP_K):
            row_copy(t, k).wait()


def _dispatch(dest, x2, init):
    t, d = x2.shape
    tm = min(MOE_TM, t)
    return pl.pallas_call(
        _dispatch_kernel,
        grid=(t // tm,),
        in_specs=[pl.BlockSpec((TOP_K, tm), lambda i: (0, i), memory_space=pltpu.SMEM),
                  pl.BlockSpec(memory_space=pl.ANY), pl.BlockSpec(memory_space=pl.ANY)],
        out_specs=pl.BlockSpec(memory_space=pl.ANY),
        out_shape=jax.ShapeDtypeStruct(init.shape, init.dtype),
        input_output_aliases={2: 0},
        scratch_shapes=[pltpu.SemaphoreType.DMA],
        compiler_params=_params(("arbitrary",)),
        name="dispatch",
    )(dest, x2, init)


def _ffn_kernel(be_ref, nvalid_ref, x_ref, wgu_ref, wd_ref, o_ref):
    i = pl.program_id(0)
    nvalid = nvalid_ref[i]

    @pl.when(nvalid > 0)
    def _():
        row = lax.broadcasted_iota(jnp.int32, (x_ref.shape[0], 1), 0)
        x = jnp.where(row < nvalid, x_ref[...], 0.0)
        gu = _mm(x, wgu_ref[...])
        hid = _silu(gu[:, 0:D_EXPERT]) * gu[:, D_EXPERT:2 * D_EXPERT]
        o_ref[...] = _mm(hid, wd_ref[...])

    @pl.when(nvalid <= 0)
    def _():
        o_ref[...] = jnp.zeros(o_ref.shape, o_ref.dtype)


def _ffn(block_expert, nvalid, xs, wgu, wd):
    npad = xs.shape[0]
    return pl.pallas_call(
        _ffn_kernel,
        grid_spec=pltpu.PrefetchScalarGridSpec(
            num_scalar_prefetch=2,
            grid=(npad // MOE_BLK,),
            in_specs=[pl.BlockSpec((MOE_BLK, D_MODEL), lambda i, be, nv: (i, 0)),
                      pl.BlockSpec((None, D_MODEL, 2 * D_EXPERT), lambda i, be, nv: (be[i], 0, 0)),
                      pl.BlockSpec((None, D_EXPERT, D_MODEL), lambda i, be, nv: (be[i], 0, 0))],
            out_specs=pl.BlockSpec((MOE_BLK, D_MODEL), lambda i, be, nv: (i, 0))),
        out_shape=jax.ShapeDtypeStruct((npad, D_MODEL), F32),
        compiler_params=_params(("arbitrary",)),
        name="ffn",
    )(block_expert, nvalid, xs, wgu, wd)


def _combine_kernel(dest_ref, x_ref, w_ref, g_ref, b_ref, yb_hbm, o_ref, y_buf, sem):
    tm = x_ref.shape[0]

    def row_copy(t, k):
        return pltpu.make_async_copy(yb_hbm.at[pl.ds(dest_ref[k, t], 1)], y_buf.at[k, pl.ds(t, 1)], sem)

    @pl.loop(0, tm)
    def _(t):
        for k in range(TOP_K):
            row_copy(t, k).start()

    @pl.loop(0, tm)
    def _(t):
        for k in range(TOP_K):
            row_copy(t, k).wait()

    ff = y_buf[0] * w_ref[:, 0:1] + y_buf[1] * w_ref[:, 1:2]
    o_ref[...] = _layer_norm(DN_ALPHA * x_ref[...] + ff, g_ref[...], b_ref[...])


def _combine(dest, x2, yb, w, g, b):
    t = x2.shape[0]
    tm = min(MOE_TM, t)
    row = lambda w_: pl.BlockSpec((tm, w_), lambda i: (i, 0))
    const = lambda shape: pl.BlockSpec(shape, lambda i: (0,) * len(shape))
    return pl.pallas_call(
        _combine_kernel,
        grid=(t // tm,),
        in_specs=[pl.BlockSpec((TOP_K, tm), lambda i: (0, i), memory_space=pltpu.SMEM),
                  row(D_MODEL), row(TOP_K), const(g.shape), const(b.shape),
                  pl.BlockSpec(memory_space=pl.ANY)],
        out_specs=row(D_MODEL),
        out_shape=jax.ShapeDtypeStruct((t, D_MODEL), F32),
        scratch_shapes=[pltpu.VMEM((TOP_K, tm, D_MODEL), F32), pltpu.SemaphoreType.DMA],
        compiler_params=_params(("arbitrary",)),
        name="combine",
    )(dest, x2, w, g, b, yb)


def _moe_rows(t):
    n_blocks = (t * TOP_K + N_EXPERTS * (MOE_BLK - 1) + MOE_BLK - 1) // MOE_BLK
    return n_blocks * MOE_BLK


def _moe(x2, slots_init, wrt, br, wgu, wd, ln_g, ln_b):
    t = x2.shape[0]
    expert, gate, rank, counts = _router(x2, wrt, br)
    counts = counts[:, 0]
    padded = (counts + MOE_BLK - 1) // MOE_BLK * MOE_BLK
    pend = jnp.cumsum(padded)
    pstart = pend - padded
    n_blocks = _moe_rows(t) // MOE_BLK
    eid = jnp.arange(N_EXPERTS, dtype=jnp.int32)[:, None, None]
    dest = jnp.sum(jnp.where(expert[None] == eid, pstart[:, None, None], 0), axis=0) + rank
    block_start = jnp.arange(n_blocks, dtype=jnp.int32) * MOE_BLK
    block_expert = jnp.minimum(jnp.sum(block_start[:, None] >= pend[None, :], axis=1), N_EXPERTS - 1)
    block_expert = block_expert.astype(jnp.int32)
    nvalid = jnp.clip((pstart + counts)[block_expert] - block_start, 0, MOE_BLK).astype(jnp.int32)
    xs = _dispatch(dest, x2, slots_init)
    yb = _ffn(block_expert, nvalid, xs, wgu, wd)
    return _combine(dest, x2, yb, gate.T, ln_g, ln_b), yb


def _prep_weights(mix_w_in, gdn_conv_w, gdn_a_log, gdn_dt_bias, ssd_conv_w, ssd_conv_b, ssd_a_log, ssd_dt_bias,
                  ssd_d, moe_w_gate, moe_w_up, xa_w_k, xa_w_v, router_w, router_b):
    nl = mix_w_in.shape[0]
    off = [0]
    for s in (GDN_QKV, GDN_HEADS * GDN_DV, GDN_HEADS, GDN_HEADS, SSD_XBC, SSD_INNER, SSD_HEADS,
              RET_HEADS * RET_DK, RET_HEADS * RET_DK, RET_HEADS * RET_DV, RET_HEADS * RET_DV,
              N_BRANCH * D_MODEL):
        off.append(off[-1] + s)
    sl = lambda i, j=None: mix_w_in[:, :, off[i]:off[(i if j is None else j) + 1]]
    zeros = lambda n: jnp.zeros((nl, D_MODEL, n), mix_w_in.dtype)
    w_all = jnp.concatenate([
        sl(11),
        sl(0),
        sl(7, 10),
        sl(4, 5),
        sl(1),
        sl(2, 3), zeros(LANE - 2 * GDN_HEADS),
        sl(6), zeros(LANE - SSD_HEADS),
        zeros(PROJ_N - (COL_SSD_DT + 1) * LANE)], axis=-1).astype(BF16)
    wabt = jnp.swapaxes(sl(2, 3), 1, 2).astype(BF16)
    wdtt = jnp.swapaxes(sl(6), 1, 2).astype(BF16)

    def col_params(a_log, dt_bias, lane0):
        n = a_log.shape[1]
        p = jnp.zeros((nl, 2, LANE), F32)
        return p.at[:, 0, lane0:lane0 + n].set(a_log).at[:, 1, lane0:lane0 + n].set(dt_bias)

    def row_params(a_log, dt_bias):
        n = a_log.shape[1]
        p = jnp.zeros((nl, 16, 2), F32)
        return p.at[:, 0:n, 0].set(a_log).at[:, 0:n, 1].set(dt_bias)

    return dict(
        w_all=w_all, wabt=wabt, wdtt=wdtt,
        gdn_pcol=col_params(gdn_a_log, gdn_dt_bias, 0), gdn_prow=row_params(gdn_a_log, gdn_dt_bias),
        ssd_pcol=col_params(ssd_a_log, ssd_dt_bias, 0), ssd_prow=row_params(ssd_a_log, ssd_dt_bias),
        ssd_dskip=jnp.repeat(ssd_d, SSD_HEADDIM, axis=-1)[:, None, :],
        wgu=jnp.concatenate([moe_w_gate, moe_w_up], axis=-1).astype(BF16),
        wkv=jnp.concatenate([xa_w_k, xa_w_v], axis=-1).astype(BF16),
        conv_w=jnp.zeros((nl, CONV_K, PROJ_N), F32)
        .at[:, :, COL_GDN * COL_W:COL_GDN * COL_W + GDN_QKV].set(gdn_conv_w)
        .at[:, :, COL_SSD * COL_W:COL_SSD * COL_W + SSD_XBC].set(ssd_conv_w),
        conv_b=jnp.zeros((nl, 1, PROJ_N), F32)
        .at[:, 0, COL_SSD * COL_W:COL_SSD * COL_W + SSD_XBC].set(ssd_conv_b),
        wrt=router_w.T.astype(BF16), br=router_b[:, None],
    )


def kernel(x, mem, mix_w_in, gdn_conv_w, gdn_a_log, gdn_dt_bias, gdn_norm_w, ssd_conv_w, ssd_conv_b, ssd_a_log, ssd_dt_bias, ssd_d, ssd_norm_w, w_proj_gdn, w_proj_ssd, w_proj_ret, mix_w_out, ln_mix_g, ln_mix_b, xa_w_q, xa_w_k, xa_w_v, xa_w_o, ln_xa_g, ln_xa_b, router_w, router_b, moe_w_gate, moe_w_up, moe_w_down, ln_moe_g, ln_moe_b):
    bsz, seq, d = x.shape
    t = bsz * seq
    p = _prep_weights(mix_w_in, gdn_conv_w, gdn_a_log, gdn_dt_bias, ssd_conv_w, ssd_conv_b, ssd_a_log,
                      ssd_dt_bias, ssd_d, moe_w_gate, moe_w_up, xa_w_k, xa_w_v, router_w, router_b)
    bf = lambda w: w.astype(BF16)
    wpg, wps, wpr, wout = bf(w_proj_gdn), bf(w_proj_ssd), bf(w_proj_ret), bf(mix_w_out)
    wq, wo, wd = bf(xa_w_q), bf(xa_w_o), bf(moe_w_down)
    vec = lambda v, l: v[l][None, :]

    pos = jnp.arange(seq, dtype=F32)
    inv_freq = ROPE_BASE ** (-jnp.arange(0, RET_DK, 2, dtype=F32) / RET_DK)
    ang = pos[:, None] * inv_freq[None, :]
    cos, sin = jnp.cos(ang), jnp.sin(ang)
    cos2 = jnp.concatenate([cos, cos], axis=-1)
    sin2 = jnp.concatenate([-sin, sin], axis=-1)
    mem2 = mem.reshape(bsz * mem.shape[1], d)

    x2 = x.reshape(t, d)
    slots = jnp.zeros((_moe_rows(t), d), F32)
    for l in range(DEPTH):
        h2 = _proj(x2, p["w_all"][l], p["conv_w"][l], p["conv_b"][l], seq)
        x3, h3 = x2.reshape(bsz, seq, d), h2.reshape(bsz, seq, PROJ_N)
        yg = _gdn(x3, h3, p["wabt"][l], p["gdn_pcol"][l], p["gdn_prow"][l], vec(gdn_norm_w, l))
        ys = _ssd(x3, h3, p["wdtt"][l], p["ssd_pcol"][l], p["ssd_prow"][l], p["ssd_dskip"][l],
                  vec(ssd_norm_w, l))
        yr = _ret(h3, cos2, sin2)
        x2 = _merge(x2, h2, yg.reshape(t, d), ys.reshape(t, d), yr.reshape(t, d), wpg[l], wps[l], wpr[l],
                    wout[l], vec(ln_mix_g, l), vec(ln_mix_b, l))
        kv = _matmul(mem2, p["wkv"][l], BF16, 512, 1024)
        x2 = _xattn(x2.reshape(bsz, seq, d), kv.reshape(bsz, mem.shape[1], 2 * d), wq[l], wo[l],
                    vec(ln_xa_g, l), vec(ln_xa_b, l)).reshape(t, d)
        x2, slots = _moe(x2, slots, p["wrt"], p["br"], p["wgu"][l], wd[l], vec(ln_moe_g, l), vec(ln_moe_b, l))
    return x2.reshape(bsz, seq, d)
```

```python
import functools
import math

import jax
import jax.numpy as jnp
from jax import lax
from jax.experimental import pallas as pl
from jax.experimental.pallas import tpu as pltpu

F32 = jnp.float32
BF16 = jnp.bfloat16

D_MODEL = 1024
DEPTH = 4
CHUNK = 64
CONV_K = 4
GDN_HEADS = 8
GDN_DK = 128
GDN_DV = 128
SSD_INNER = D_MODEL
SSD_HEADDIM = 64
SSD_HEADS = SSD_INNER // SSD_HEADDIM
SSD_GROUPS = 4
SSD_HPG = SSD_HEADS // SSD_GROUPS
SSD_STATE = 128
RET_HEADS = 4
RET_DK = 128
RET_DV = 256
ROPE_BASE = 10000.0
N_BRANCH = 3
MEM_LEN = 256
XA_HEADS = 4
XA_DH = D_MODEL // XA_HEADS
N_EXPERTS = 16
N_EXPERT_GROUPS = 4
EXPERTS_PER_GROUP = N_EXPERTS // N_EXPERT_GROUPS
TOP_K = 2
D_EXPERT = 512
DN_ALPHA = (2 * DEPTH) ** 0.25
LN_EPS = 1e-5
RMS_EPS = 1e-6

GDN_QKV = 2 * GDN_HEADS * GDN_DK + GDN_HEADS * GDN_DV
SSD_XBC = SSD_INNER + 2 * SSD_GROUPS * SSD_STATE

LANE = 128
COL_W = 3072
COL_GATE = 0
COL_GDN = 1
COL_RET = 2
COL_SSD = 3
COL_GDN_Z = 12
COL_GDN_AB = 104
COL_SSD_DT = 105
PROJ_N = 13824
PROJ_TM = 1024
PROJ_TN = 1536
PROJ_HALO = 16
PROJ_CONV_TILES = (COL_GDN * COL_W // PROJ_TN, COL_GDN * COL_W // PROJ_TN + 1, COL_SSD * COL_W // PROJ_TN)
PROJ_CONV_PART_TILE = COL_SSD * COL_W // PROJ_TN + 1
PROJ_CONV_PART_COLS = SSD_XBC - PROJ_TN
PROJ_VMEM_LIMIT = 56 * 1024 * 1024
GDN_TILE = 256
SSD_TILE = 256
RET_TILE = 256

MOE_BLK = 512
MOE_TM = 512
VMEM_LIMIT = 48 * 1024 * 1024


def _mm(a, b):
    return jnp.dot(a.astype(BF16), b.astype(BF16), preferred_element_type=F32)


def _mm_nt(a, b):
    return lax.dot_general(a.astype(BF16), b.astype(BF16), (((1,), (1,)), ((), ())),
                           preferred_element_type=F32)


def _mm_tn(a, b):
    return lax.dot_general(a.astype(BF16), b.astype(BF16), (((0,), (0,)), ((), ())),
                           preferred_element_type=F32)


def _bf16_pieces(a):
    hi = a.astype(BF16)
    r1 = a - hi.astype(F32)
    mid = r1.astype(BF16)
    lo = (r1 - mid.astype(F32)).astype(BF16)
    return hi, mid, lo


def _mm_exact_01(a, sel):
    m = a.shape[0]
    p = jnp.dot(jnp.concatenate(_bf16_pieces(a), axis=0), sel.astype(BF16), preferred_element_type=F32)
    return p[0:m] + p[m:2 * m] + p[2 * m:3 * m]


def _mm_01_exact(sel, a):
    n = a.shape[1]
    p = jnp.dot(sel.astype(BF16), jnp.concatenate(_bf16_pieces(a), axis=1), preferred_element_type=F32)
    return p[:, 0:n] + p[:, n:2 * n] + p[:, 2 * n:3 * n]


def _silu(x):
    hx = 0.5 * x
    return hx + hx * jnp.tanh(hx)


def _softplus(x):
    return jnp.maximum(x, 0.0) + jnp.log1p(jnp.exp(-jnp.abs(x)))


def _layer_norm(v, g, b):
    mu = jnp.mean(v, -1, keepdims=True)
    var = jnp.mean(jnp.square(v - mu), -1, keepdims=True)
    return (v - mu) * lax.rsqrt(var + LN_EPS) * g + b


def _params(sem):
    return pltpu.CompilerParams(dimension_semantics=sem, vmem_limit_bytes=VMEM_LIMIT)


def _matmul_kernel(x_ref, w_ref, o_ref):
    o_ref[...] = _mm(x_ref[...], w_ref[...]).astype(o_ref.dtype)


def _matmul(x, w, out_dtype, tm, tn):
    m, k = x.shape
    n = w.shape[1]
    tm, tn = min(tm, m), min(tn, n)
    return pl.pallas_call(
        _matmul_kernel,
        grid=(m // tm, n // tn),
        in_specs=[pl.BlockSpec((tm, k), lambda i, j: (i, 0)),
                  pl.BlockSpec((k, tn), lambda i, j: (0, j))],
        out_specs=pl.BlockSpec((tm, tn), lambda i, j: (i, j)),
        out_shape=jax.ShapeDtypeStruct((m, n), out_dtype),
        compiler_params=_params(("parallel", "arbitrary")),
        name="matmul",
    )(x, w)


def _proj_kernel(x_ref, xprev_ref, w_ref, cw_ref, cb_ref, o_ref, xs_ref, res_ref, *, tiles_per_seq):
    i, j = pl.program_id(0), pl.program_id(1)
    tm, tn = o_ref.shape

    @pl.when(j == 0)
    def _():
        xs_ref[PROJ_HALO:PROJ_HALO + tm, :] = x_ref[...].astype(BF16)
        starts_sequence = (i % tiles_per_seq) == 0
        xs_ref[0:PROJ_HALO, :] = jnp.where(starts_sequence, 0.0, xprev_ref[...]).astype(BF16)

    r = jnp.dot(xs_ref[...], w_ref[...], preferred_element_type=F32)

    def conv_silu(ncols):
        res_ref[:, 0:ncols] = r[:, 0:ncols]
        acc = r[PROJ_HALO:, 0:ncols] * cw_ref[CONV_K - 1:CONV_K, 0:ncols] + cb_ref[:, 0:ncols]
        for tap in range(CONV_K - 1):
            off = PROJ_HALO - (CONV_K - 1) + tap
            acc = acc + res_ref[off:off + tm, 0:ncols] * cw_ref[tap:tap + 1, 0:ncols]
        return _silu(acc)

    conv_full = (j == PROJ_CONV_TILES[0]) | (j == PROJ_CONV_TILES[1]) | (j == PROJ_CONV_TILES[2])
    conv_part = j == PROJ_CONV_PART_TILE

    @pl.when(conv_full)
    def _():
        o_ref[...] = conv_silu(tn)

    @pl.when(conv_part)
    def _():
        o_ref[:, 0:PROJ_CONV_PART_COLS] = conv_silu(PROJ_CONV_PART_COLS)
        o_ref[:, PROJ_CONV_PART_COLS:] = r[PROJ_HALO:, PROJ_CONV_PART_COLS:]

    @pl.when(jnp.logical_not(conv_full | conv_part))
    def _():
        o_ref[...] = r[PROJ_HALO:, :]


def _proj(x2, w, cw, cb, seq):
    t, d = x2.shape
    tm = min(PROJ_TM, seq)
    tn = PROJ_TN
    halo_blocks = tm // PROJ_HALO
    return pl.pallas_call(
        functools.partial(_proj_kernel, tiles_per_seq=seq // tm),
        grid=(t // tm, PROJ_N // tn),
        in_specs=[pl.BlockSpec((tm, d), lambda i, j: (i, 0)),
                  pl.BlockSpec((PROJ_HALO, d), lambda i, j: (jnp.maximum(i * halo_blocks - 1, 0), 0)),
                  pl.BlockSpec((d, tn), lambda i, j: (0, j)),
                  pl.BlockSpec((CONV_K, tn), lambda i, j: (0, j)),
                  pl.BlockSpec((1, tn), lambda i, j: (0, j))],
        out_specs=pl.BlockSpec((tm, tn), lambda i, j: (i, j)),
        out_shape=jax.ShapeDtypeStruct((t, PROJ_N), F32),
        scratch_shapes=[pltpu.VMEM((PROJ_HALO + tm, d), BF16),
                        pltpu.VMEM((PROJ_HALO + tm, tn), F32)],
        compiler_params=pltpu.CompilerParams(dimension_semantics=("parallel", "arbitrary"),
                                             vmem_limit_bytes=PROJ_VMEM_LIMIT),
        name="proj",
    )(x2, x2, w, cw, cb)


def _tri_masks(n):
    row = lax.broadcasted_iota(jnp.int32, (n, n), 0)
    col = lax.broadcasted_iota(jnp.int32, (n, n), 1)
    return row >= col, row > col


def _upper_ones(n):
    row = lax.broadcasted_iota(jnp.int32, (n, n), 0)
    col = lax.broadcasted_iota(jnp.int32, (n, n), 1)
    return (row <= col).astype(F32)


def _decay_matrix(cum_c, cum_r, causal):
    return jnp.exp(jnp.where(causal, cum_c - cum_r, -jnp.inf))


def _chunk_rows(c):
    return slice(c * CHUNK, (c + 1) * CHUNK)


def _chunk_cumsums(la_col, la_row, nch):
    tri = _tri_masks(CHUNK)[0].astype(F32)
    upper = _upper_ones(CHUNK)
    cum_col = [_mm_01_exact(tri, la_col[_chunk_rows(c), :]) for c in range(nch)]
    cum_row = [_mm_exact_01(la_row[:, _chunk_rows(c)], upper) for c in range(nch)]
    return cum_col, cum_row


def _gdn_kernel(x_ref, qkv_ref, z_ref, ab_ref, wabt_ref, pcol_ref, prow_ref, normw_ref, o_ref, state_ref):
    L = qkv_ref.shape[0]
    C = CHUNK
    nch = L // C
    H = range(GDN_HEADS)

    @pl.when(pl.program_id(1) == 0)
    def _():
        state_ref[...] = jnp.zeros(state_ref.shape, F32)

    causal, strict = _tri_masks(C)
    eye = (causal & ~strict).astype(F32)

    ab = ab_ref[...]
    la_col = -jnp.exp(pcol_ref[0:1, :]) * _softplus(ab + pcol_ref[1:2, :])
    beta_col = jax.nn.sigmoid(ab)
    ab_t = _mm_nt(wabt_ref[...], x_ref[...])
    la_row = -jnp.exp(prow_ref[:, 0:1]) * _softplus(ab_t + prow_ref[:, 1:2])
    cum_col, cum_row = _chunk_cumsums(la_col, la_row, nch)

    P = [(c, h) for c in range(nch) for h in H]
    I = range(len(P))
    k_off, v_off = GDN_HEADS * GDN_DK, 2 * GDN_HEADS * GDN_DK
    qs = [qkv_ref[_chunk_rows(c), h * GDN_DK:(h + 1) * GDN_DK] for c, h in P]
    ks = [qkv_ref[_chunk_rows(c), k_off + h * GDN_DK:k_off + (h + 1) * GDN_DK] for c, h in P]
    vs = [qkv_ref[_chunk_rows(c), v_off + h * GDN_DV:v_off + (h + 1) * GDN_DV] for c, h in P]
    qs = [q * (lax.rsqrt(jnp.sum(jnp.square(q), -1, keepdims=True) + RMS_EPS) * GDN_DK ** -0.5) for q in qs]
    ks = [k * lax.rsqrt(jnp.sum(jnp.square(k), -1, keepdims=True) + RMS_EPS) for k in ks]
    cum_c = [cum_col[c][:, h:h + 1] for c, h in P]
    beta = [beta_col[_chunk_rows(c), GDN_HEADS + h:GDN_HEADS + h + 1] for c, h in P]
    decay = [_decay_matrix(cum_c[i], cum_row[c][h:h + 1, :], causal) for i, (c, h) in enumerate(P)]
    kb = [ks[i] * beta[i] for i in I]
    vb = [vs[i] * beta[i] for i in I]
    kq_k = [_mm_nt(jnp.concatenate([kb[i], qs[i]], axis=0), ks[i]) for i in I]
    n_pow = [-jnp.where(strict, kq_k[i][0:C] * decay[i], 0.0) for i in I]
    attn = [jnp.where(causal, kq_k[i][C:2 * C] * decay[i], 0.0) for i in I]
    t_mat = [eye + n_pow[i] for i in I]
    n_pow = [_mm(n_pow[i], n_pow[i]) for i in I]
    for _ in range(int(math.log2(C)) - 2):
        both = [_mm(jnp.concatenate([t_mat[i], n_pow[i]], axis=0), n_pow[i]) for i in I]
        t_mat = [t_mat[i] + both[i][0:C] for i in I]
        n_pow = [both[i][C:2 * C] for i in I]
    upd = [_mm(t_mat[i], n_pow[i]) for i in I]
    t_mat = [t_mat[i] + upd[i] for i in I]
    e_cum = [jnp.exp(cum_c[i]) for i in I]
    uw = [_mm(t_mat[i], jnp.concatenate([vb[i], kb[i] * e_cum[i]], axis=-1)) for i in I]
    u = [uw[i][:, 0:GDN_DV] for i in I]
    w = [uw[i][:, GDN_DV:] for i in I]
    last = [cum_c[i][C - 1:C, :] for i in I]
    wq = [jnp.concatenate([w[i], qs[i] * e_cum[i]], axis=0) for i in I]
    ak = [jnp.concatenate([attn[i], (ks[i] * jnp.exp(last[i] - cum_c[i])).T], axis=0) for i in I]

    states = [state_ref[h] for h in H]
    outs = []
    for c in range(nch):
        ix = [c * GDN_HEADS + h for h in H]
        wq_s = [_mm(wq[i], states[h]) for h, i in zip(H, ix)]
        v_new = [u[i] - wq_s[h][0:C] for h, i in zip(H, ix)]
        ak_v = [_mm(ak[i], v_new[h]) for h, i in zip(H, ix)]
        states = [states[h] * jnp.exp(last[i]) + ak_v[h][C:] for h, i in zip(H, ix)]
        outs += [wq_s[h][C:2 * C] + ak_v[h][0:C] for h in H]
    for h in H:
        state_ref[h] = states[h]
    scale = [lax.rsqrt(jnp.mean(jnp.square(o), -1, keepdims=True) + RMS_EPS) for o in outs]
    gate = [_silu(z_ref[_chunk_rows(c), h * GDN_DV:(h + 1) * GDN_DV]) for c, h in P]
    for i, (c, h) in enumerate(P):
        o_ref[_chunk_rows(c), h * GDN_DV:(h + 1) * GDN_DV] = (
            outs[i] * scale[i] * normw_ref[...] * gate[i]).astype(o_ref.dtype)


def _gdn(x3, h3, wabt, pcol, prow, normw):
    bsz, seq, _ = x3.shape
    L = min(GDN_TILE, seq)
    const = lambda shape: pl.BlockSpec(shape, lambda b, c: (0,) * len(shape))
    return pl.pallas_call(
        _gdn_kernel,
        grid=(bsz, seq // L),
        in_specs=[pl.BlockSpec((None, L, D_MODEL), lambda b, c: (b, c, 0)),
                  pl.BlockSpec((None, L, COL_W), lambda b, c: (b, c, COL_GDN)),
                  pl.BlockSpec((None, L, D_MODEL), lambda b, c: (b, c, COL_GDN_Z)),
                  pl.BlockSpec((None, L, LANE), lambda b, c: (b, c, COL_GDN_AB)),
                  const(wabt.shape), const(pcol.shape), const(prow.shape), const(normw.shape)],
        out_specs=pl.BlockSpec((None, L, D_MODEL), lambda b, c: (b, c, 0)),
        out_shape=jax.ShapeDtypeStruct((bsz, seq, GDN_HEADS * GDN_DV), BF16),
        scratch_shapes=[pltpu.VMEM((GDN_HEADS, GDN_DK, GDN_DV), F32)],
        compiler_params=_params(("parallel", "arbitrary")),
        name="gdn",
    )(x3, h3, h3, h3, wabt, pcol, prow, normw)


def _ssd_kernel(x_ref, xbcz_ref, dt_ref, wdtt_ref, pcol_ref, prow_ref, dskip_ref, normw_ref, o_ref, state_ref):
    L = xbcz_ref.shape[0]
    C = CHUNK
    nch = L // C
    G = range(SSD_GROUPS)
    R = range(SSD_HPG)
    gw = SSD_HPG * SSD_HEADDIM
    b_off, c_off = SSD_INNER, SSD_INNER + SSD_GROUPS * SSD_STATE
    cols = lambda g: slice(g * gw, (g + 1) * gw)

    @pl.when(pl.program_id(1) == 0)
    def _():
        state_ref[...] = jnp.zeros(state_ref.shape, F32)

    causal, _ = _tri_masks(C)
    dt_col = _softplus(dt_ref[...] + pcol_ref[1:2, :])
    la_col = -jnp.exp(pcol_ref[0:1, :]) * dt_col
    dt_t = _mm_nt(wdtt_ref[...], x_ref[...])
    la_row = -jnp.exp(prow_ref[:, 0:1]) * _softplus(dt_t + prow_ref[:, 1:2])
    cum_col, cum_row = _chunk_cumsums(la_col, la_row, nch)

    expand = (lax.broadcasted_iota(jnp.int32, (LANE, SSD_INNER), 1) // SSD_HEADDIM
              == lax.broadcasted_iota(jnp.int32, (LANE, SSD_INNER), 0)).astype(F32)
    wide = [_mm_exact_01(jnp.concatenate([cum_col[c], dt_col[_chunk_rows(c), :]], axis=0), expand)
            for c in range(nch)]
    cum_w = [wide[c][0:C] for c in range(nch)]
    last_w = [cum_w[c][C - 1:C, :] for c in range(nch)]
    v = [xbcz_ref[_chunk_rows(c), 0:SSD_INNER] * wide[c][C:2 * C] for c in range(nch)]
    v_k = [v[c] * jnp.exp(last_w[c] - cum_w[c]) for c in range(nch)]

    PG = [(c, g) for c in range(nch) for g in G]
    bm = {(c, g): xbcz_ref[_chunk_rows(c), b_off + g * SSD_STATE:b_off + (g + 1) * SSD_STATE] for c, g in PG}
    cm = {(c, g): xbcz_ref[_chunk_rows(c), c_off + g * SSD_STATE:c_off + (g + 1) * SSD_STATE] for c, g in PG}
    scores = {p: _mm_nt(cm[p], bm[p]) for p in PG}
    head = lambda g, r: g * SSD_HPG + r
    decay = {(c, g, r): _decay_matrix(cum_col[c][:, head(g, r):head(g, r) + 1],
                                      cum_row[c][head(g, r):head(g, r) + 1, :], causal)
             for c, g in PG for r in R}
    stacked = {(c, g): _mm(jnp.concatenate([scores[c, g] * decay[c, g, r] for r in R], axis=0), v[c][:, cols(g)])
               for c, g in PG}
    lane_head = lax.broadcasted_iota(jnp.int32, (1, gw), 1) // SSD_HEADDIM
    y_intra = {p: functools.reduce(lambda acc, r: jnp.where(lane_head == r, stacked[p][r * C:(r + 1) * C], acc),
                                   R[1:], stacked[p][0:C]) for p in PG}
    k_v = {(c, g): _mm_tn(bm[c, g], v_k[c][:, cols(g)]) for c, g in PG}

    states = [state_ref[g] for g in G]
    y_state = {}
    for c in range(nch):
        for g in G:
            y_state[c, g] = _mm(cm[c, g], states[g])
        e_last = jnp.exp(last_w[c])
        states = [states[g] * e_last[:, cols(g)] + k_v[c, g] for g in G]
    for g in G:
        state_ref[g] = states[g]
    e_cum = [jnp.exp(cum_w[c]) for c in range(nch)]
    y = {(c, g): (y_intra[c, g] + y_state[c, g] * e_cum[c][:, cols(g)]
                  + dskip_ref[:, cols(g)] * xbcz_ref[_chunk_rows(c), cols(g)]) for c, g in PG}
    y = {(c, g): y[c, g] * _silu(xbcz_ref[_chunk_rows(c), SSD_XBC + g * gw:SSD_XBC + (g + 1) * gw]) for c, g in PG}
    scale = {p: lax.rsqrt(jnp.mean(jnp.square(y[p]), -1, keepdims=True) + RMS_EPS) for p in PG}
    for c, g in PG:
        o_ref[_chunk_rows(c), cols(g)] = (y[c, g] * scale[c, g] * normw_ref[:, cols(g)]).astype(o_ref.dtype)


def _ssd(x3, h3, wdtt, pcol, prow, dskip, normw):
    bsz, seq, _ = x3.shape
    L = min(SSD_TILE, seq)
    const = lambda shape: pl.BlockSpec(shape, lambda b, c: (0,) * len(shape))
    return pl.pallas_call(
        _ssd_kernel,
        grid=(bsz, seq // L),
        in_specs=[pl.BlockSpec((None, L, D_MODEL), lambda b, c: (b, c, 0)),
                  pl.BlockSpec((None, L, COL_W), lambda b, c: (b, c, COL_SSD)),
                  pl.BlockSpec((None, L, LANE), lambda b, c: (b, c, COL_SSD_DT)),
                  const(wdtt.shape), const(pcol.shape), const(prow.shape), const(dskip.shape),
                  const(normw.shape)],
        out_specs=pl.BlockSpec((None, L, D_MODEL), lambda b, c: (b, c, 0)),
        out_shape=jax.ShapeDtypeStruct((bsz, seq, SSD_INNER), BF16),
        scratch_shapes=[pltpu.VMEM((SSD_GROUPS, SSD_STATE, SSD_HPG * SSD_HEADDIM), F32)],
        compiler_params=_params(("parallel", "arbitrary")),
        name="ssd",
    )(x3, h3, h3, wdtt, pcol, prow, dskip, normw)


def _ret_kernel(qkvg_ref, cos_ref, sin_ref, o_ref, state_ref):
    L = qkvg_ref.shape[0]
    C = CHUNK
    nch = L // C
    H = range(RET_HEADS)

    @pl.when(pl.program_id(1) == 0)
    def _():
        state_ref[...] = jnp.zeros(state_ref.shape, F32)

    causal, _ = _tri_masks(C)
    row = lax.broadcasted_iota(jnp.int32, (C, C), 0)
    col = lax.broadcasted_iota(jnp.int32, (C, C), 1)
    dist = (row - col).astype(F32)
    pos = lax.broadcasted_iota(jnp.int32, (C, 1), 0).astype(F32)
    log_gamma = [math.log1p(-2.0 ** (-5.0 - h)) for h in H]
    decay = [jnp.exp(jnp.where(causal, dist * log_gamma[h], -jnp.inf)) for h in H]
    e_cum = [jnp.exp((pos + 1.0) * log_gamma[h]) for h in H]
    e_rest = [jnp.exp((C - 1.0 - pos) * log_gamma[h]) for h in H]
    k_off = RET_HEADS * RET_DK
    v_off = 2 * RET_HEADS * RET_DK
    g_off = v_off + RET_HEADS * RET_DV

    P = [(c, h) for c in range(nch) for h in H]
    cos2 = {c: cos_ref[_chunk_rows(c), :] for c in range(nch)}
    sin2 = {c: sin_ref[_chunk_rows(c), :] for c in range(nch)}
    rot = lambda t, c: t * cos2[c] + pltpu.roll(t, RET_DK // 2, 1) * sin2[c]
    q = {(c, h): rot(qkvg_ref[_chunk_rows(c), h * RET_DK:(h + 1) * RET_DK], c) for c, h in P}
    k = {(c, h): rot(qkvg_ref[_chunk_rows(c), k_off + h * RET_DK:k_off + (h + 1) * RET_DK], c) * RET_DK ** -0.5
         for c, h in P}
    v = {(c, h): qkvg_ref[_chunk_rows(c), v_off + h * RET_DV:v_off + (h + 1) * RET_DV] for c, h in P}
    scores = {p: _mm_nt(q[p], k[p]) for p in P}
    y_intra = {(c, h): _mm(scores[c, h] * decay[h], v[c, h]) for c, h in P}
    k_v = {(c, h): _mm_tn(k[c, h], v[c, h] * e_rest[h]) for c, h in P}

    states = [state_ref[h] for h in H]
    for c in range(nch):
        y_state = [_mm(q[c, h], states[h]) for h in H]
        states = [states[h] * math.exp(C * log_gamma[h]) + k_v[c, h] for h in H]
        for h in H:
            o = y_intra[c, h] + y_state[h] * e_cum[h]
            mu = jnp.mean(o, -1, keepdims=True)
            var = jnp.mean(jnp.square(o - mu), -1, keepdims=True)
            o = (o - mu) * lax.rsqrt(var + LN_EPS)
            o = _silu(qkvg_ref[_chunk_rows(c), g_off + h * RET_DV:g_off + (h + 1) * RET_DV]) * o
            o_ref[_chunk_rows(c), h * RET_DV:(h + 1) * RET_DV] = o.astype(o_ref.dtype)
    for h in H:
        state_ref[h] = states[h]


def _ret(h3, cos2, sin2):
    bsz, seq, _ = h3.shape
    L = min(RET_TILE, seq)
    return pl.pallas_call(
        _ret_kernel,
        grid=(bsz, seq // L),
        in_specs=[pl.BlockSpec((None, L, COL_W), lambda b, c: (b, c, COL_RET)),
                  pl.BlockSpec((L, RET_DK), lambda b, c: (c, 0)),
                  pl.BlockSpec((L, RET_DK), lambda b, c: (c, 0))],
        out_specs=pl.BlockSpec((None, L, D_MODEL), lambda b, c: (b, c, 0)),
        out_shape=jax.ShapeDtypeStruct((bsz, seq, RET_HEADS * RET_DV), BF16),
        scratch_shapes=[pltpu.VMEM((RET_HEADS, RET_DK, RET_DV), F32)],
        compiler_params=_params(("parallel", "arbitrary")),
        name="ret",
    )(h3, cos2, sin2)


def _merge_kernel(x_ref, gate_ref, yg_ref, ys_ref, yr_ref, wg_ref, ws_ref, wr_ref, wo_ref, g_ref, b_ref,
                  o_ref):
    d = D_MODEL
    merged = (jax.nn.sigmoid(gate_ref[:, 0:d]) * _mm(yg_ref[...], wg_ref[...])
              + jax.nn.sigmoid(gate_ref[:, d:2 * d]) * _mm(ys_ref[...], ws_ref[...])
              + jax.nn.sigmoid(gate_ref[:, 2 * d:3 * d]) * _mm(yr_ref[...], wr_ref[...]))
    mix = _mm(merged, wo_ref[...])
    o_ref[...] = _layer_norm(DN_ALPHA * x_ref[...] + mix, g_ref[...], b_ref[...])


def _merge(x2, h2, yg, ys, yr, wg, ws, wr, wo, g, b, tm=512):
    t = x2.shape[0]
    tm = min(tm, t)
    row = lambda w: pl.BlockSpec((tm, w), lambda i: (i, 0))
    const = lambda shape: pl.BlockSpec(shape, lambda i: (0,) * len(shape))
    return pl.pallas_call(
        _merge_kernel,
        grid=(t // tm,),
        in_specs=[row(D_MODEL), pl.BlockSpec((tm, COL_W), lambda i: (i, COL_GATE)),
                  row(D_MODEL), row(D_MODEL), row(D_MODEL),
                  const(wg.shape), const(ws.shape), const(wr.shape), const(wo.shape),
                  const(g.shape), const(b.shape)],
        out_specs=row(D_MODEL),
        out_shape=jax.ShapeDtypeStruct((t, D_MODEL), F32),
        compiler_params=_params(("parallel",)),
        name="merge",
    )(x2, h2, yg, ys, yr, wg, ws, wr, wo, g, b)


def _xattn_kernel(x_ref, kv_ref, wq_ref, wo_ref, g_ref, b_ref, o_ref):
    x = x_ref[...]
    q = _mm(x, wq_ref[...])
    H = range(XA_HEADS)
    hc = lambda h: slice(h * XA_DH, (h + 1) * XA_DH)
    s = [_mm_nt(q[:, hc(h)], kv_ref[:, hc(h)]) * XA_DH ** -0.5 for h in H]
    e = [jnp.exp(s[h] - jnp.max(s[h], -1, keepdims=True)) for h in H]
    p = [e[h] / jnp.sum(e[h], -1, keepdims=True) for h in H]
    outs = [_mm(p[h], kv_ref[:, D_MODEL + h * XA_DH:D_MODEL + (h + 1) * XA_DH]) for h in H]
    xa = _mm(jnp.concatenate(outs, axis=-1), wo_ref[...])
    o_ref[...] = _layer_norm(DN_ALPHA * x + xa, g_ref[...], b_ref[...])


def _xattn(x3, kv3, wq, wo, g, b, tm=512):
    bsz, seq, _ = x3.shape
    tm = min(tm, seq)
    const = lambda shape: pl.BlockSpec(shape, lambda bb, i: (0,) * len(shape))
    return pl.pallas_call(
        _xattn_kernel,
        grid=(bsz, seq // tm),
        in_specs=[pl.BlockSpec((None, tm, D_MODEL), lambda bb, i: (bb, i, 0)),
                  pl.BlockSpec((None, MEM_LEN, 2 * D_MODEL), lambda bb, i: (bb, 0, 0)),
                  const(wq.shape), const(wo.shape), const(g.shape), const(b.shape)],
        out_specs=pl.BlockSpec((None, tm, D_MODEL), lambda bb, i: (bb, i, 0)),
        out_shape=jax.ShapeDtypeStruct((bsz, seq, D_MODEL), F32),
        compiler_params=_params(("parallel", "parallel")),
        name="xattn",
    )(x3, kv3, wq, wo, g, b)


def _top2_sum(a, b, c, d):
    return jnp.maximum(jnp.maximum(a, b) + jnp.maximum(c, d), jnp.maximum(a + b, c + d))


def _router_kernel(x_ref, wrt_ref, br_ref, e_ref, w_ref, rank_ref, cnt_ref, carry_ref):
    tm = x_ref.shape[0]
    epg = EXPERTS_PER_GROUP

    @pl.when(pl.program_id(0) == 0)
    def _():
        carry_ref[...] = jnp.zeros(carry_ref.shape, F32)

    scores = jax.nn.sigmoid(_mm_nt(wrt_ref[...], x_ref[...]))
    sel = scores + br_ref[...]
    rows = [sel[i:i + 1, :] for i in range(N_EXPERTS)]
    srow = [scores[i:i + 1, :] for i in range(N_EXPERTS)]
    gscore = [_top2_sum(*rows[g * epg:(g + 1) * epg]) for g in range(N_EXPERT_GROUPS)]
    best, gidx = gscore[0], jnp.zeros((1, tm), jnp.int32)
    for g in range(1, N_EXPERT_GROUPS):
        better = gscore[g] > best
        best = jnp.where(better, gscore[g], best)
        gidx = jnp.where(better, g, gidx)
    pick = lambda vals, j: functools.reduce(
        lambda acc, g: jnp.where(gidx == g, vals[g * epg + j], acc), range(1, N_EXPERT_GROUPS), vals[j])
    ing = [pick(rows, j) for j in range(epg)]
    ins = [pick(srow, j) for j in range(epg)]
    v0, l0, s0 = ing[0], jnp.zeros((1, tm), jnp.int32), ins[0]
    for j in range(1, epg):
        better = ing[j] > v0
        v0 = jnp.where(better, ing[j], v0)
        l0 = jnp.where(better, j, l0)
        s0 = jnp.where(better, ins[j], s0)
    v1 = jnp.full((1, tm), -jnp.inf, F32)
    l1 = jnp.zeros((1, tm), jnp.int32)
    s1 = jnp.zeros((1, tm), F32)
    for j in range(epg):
        better = (ing[j] > v1) & (l0 != j)
        v1 = jnp.where(better, ing[j], v1)
        l1 = jnp.where(better, j, l1)
        s1 = jnp.where(better, ins[j], s1)
    e0 = gidx * epg + l0
    e1 = gidx * epg + l1
    tot = s0 + s1
    e_ref[0:1, :] = e0
    e_ref[1:2, :] = e1
    w_ref[0:1, :] = s0 / tot
    w_ref[1:2, :] = s1 / tot
    eid = lax.broadcasted_iota(jnp.int32, (N_EXPERTS, tm), 0)
    oh0 = (eid == e0).astype(F32)
    oh1 = (eid == e1).astype(F32)
    ti = lax.broadcasted_iota(jnp.int32, (tm, tm), 0)
    tj = lax.broadcasted_iota(jnp.int32, (tm, tm), 1)
    before = _mm(oh0 + oh1, (ti < tj).astype(F32)) + carry_ref[...]
    rank_ref[0:1, :] = jnp.sum(oh0 * before, 0, keepdims=True).astype(jnp.int32)
    rank_ref[1:2, :] = jnp.sum(oh1 * before, 0, keepdims=True).astype(jnp.int32)
    carry_ref[...] = carry_ref[...] + jnp.sum(oh0 + oh1, 1, keepdims=True)
    cnt_ref[...] = carry_ref[...].astype(jnp.int32)


def _router(x2, wrt, br, tm=512):
    t = x2.shape[0]
    tm = min(tm, t)
    const = lambda shape: pl.BlockSpec(shape, lambda i: (0,) * len(shape))
    tok = pl.BlockSpec((TOP_K, tm), lambda i: (0, i))
    return pl.pallas_call(
        _router_kernel,
        grid=(t // tm,),
        in_specs=[pl.BlockSpec((tm, D_MODEL), lambda i: (i, 0)), const(wrt.shape), const(br.shape)],
        out_specs=[tok, tok, tok, const((N_EXPERTS, 1))],
        out_shape=[jax.ShapeDtypeStruct((TOP_K, t), jnp.int32), jax.ShapeDtypeStruct((TOP_K, t), F32),
                   jax.ShapeDtypeStruct((TOP_K, t), jnp.int32),
                   jax.ShapeDtypeStruct((N_EXPERTS, 1), jnp.int32)],
        scratch_shapes=[pltpu.VMEM((N_EXPERTS, 1), F32)],
        compiler_params=_params(("arbitrary",)),
        name="router",
    )(x2, wrt, br)


def _dispatch_kernel(dest_ref, x_ref, init_hbm, xs_hbm, sem):
    del init_hbm
    tm = x_ref.shape[0]

    def row_copy(t, k):
        return pltpu.make_async_copy(x_ref.at[pl.ds(t, 1)], xs_hbm.at[pl.ds(dest_ref[k, t], 1)], sem)

    @pl.loop(0, tm)
    def _(t):
        for k in range(TOP_K):
            row_copy(t, k).start()

    @pl.loop(0, tm)
    def _(t):
        for k in range(TOP_K):
            row_copy(t, k).wait()


def _dispatch(dest, x2, init):
    t, d = x2.shape
    tm = min(MOE_TM, t)
    return pl.pallas_call(
        _dispatch_kernel,
        grid=(t // tm,),
        in_specs=[pl.BlockSpec((TOP_K, tm), lambda i: (0, i), memory_space=pltpu.SMEM),
                  pl.BlockSpec((tm, d), lambda i: (i, 0)), pl.BlockSpec(memory_space=pl.ANY)],
        out_specs=pl.BlockSpec(memory_space=pl.ANY),
        out_shape=jax.ShapeDtypeStruct(init.shape, init.dtype),
        input_output_aliases={2: 0},
        scratch_shapes=[pltpu.SemaphoreType.DMA],
        compiler_params=_params(("arbitrary",)),
        name="dispatch",
    )(dest, x2, init)


def _ffn_kernel(be_ref, nvalid_ref, x_ref, wgu_ref, wd_ref, o_ref):
    i = pl.program_id(0)
    nvalid = nvalid_ref[i]

    @pl.when(nvalid > 0)
    def _():
        row = lax.broadcasted_iota(jnp.int32, (x_ref.shape[0], 1), 0)
        x = jnp.where(row < nvalid, x_ref[...], 0.0)
        gu = _mm(x, wgu_ref[...])
        hid = _silu(gu[:, 0:D_EXPERT]) * gu[:, D_EXPERT:2 * D_EXPERT]
        o_ref[...] = _mm(hid, wd_ref[...])

    @pl.when(nvalid <= 0)
    def _():
        o_ref[...] = jnp.zeros(o_ref.shape, o_ref.dtype)


def _ffn(block_expert, nvalid, xs, wgu, wd):
    npad = xs.shape[0]
    return pl.pallas_call(
        _ffn_kernel,
        grid_spec=pltpu.PrefetchScalarGridSpec(
            num_scalar_prefetch=2,
            grid=(npad // MOE_BLK,),
            in_specs=[pl.BlockSpec((MOE_BLK, D_MODEL), lambda i, be, nv: (i, 0)),
                      pl.BlockSpec((None, D_MODEL, 2 * D_EXPERT), lambda i, be, nv: (be[i], 0, 0)),
                      pl.BlockSpec((None, D_EXPERT, D_MODEL), lambda i, be, nv: (be[i], 0, 0))],
            out_specs=pl.BlockSpec((MOE_BLK, D_MODEL), lambda i, be, nv: (i, 0))),
        out_shape=jax.ShapeDtypeStruct((npad, D_MODEL), F32),
        compiler_params=_params(("arbitrary",)),
        name="ffn",
    )(block_expert, nvalid, xs, wgu, wd)


def _combine_kernel(dest_ref, x_ref, w_ref, g_ref, b_ref, yb_hbm, o_ref, y_buf, sem):
    tm = x_ref.shape[0]

    def row_copy(t, k):
        return pltpu.make_async_copy(yb_hbm.at[pl.ds(dest_ref[k, t], 1)], y_buf.at[k, pl.ds(t, 1)], sem)

    @pl.loop(0, tm)
    def _(t):
        for k in range(TOP_K):
            row_copy(t, k).start()

    @pl.loop(0, tm)
    def _(t):
        for k in range(TOP_K):
            row_copy(t, k).wait()

    ff = y_buf[0] * w_ref[:, 0:1] + y_buf[1] * w_ref[:, 1:2]
    o_ref[...] = _layer_norm(DN_ALPHA * x_ref[...] + ff, g_ref[...], b_ref[...])


def _combine(dest, x2, yb, w, g, b):
    t = x2.shape[0]
    tm = min(MOE_TM, t)
    row = lambda w_: pl.BlockSpec((tm, w_), lambda i: (i, 0))
    const = lambda shape: pl.BlockSpec(shape, lambda i: (0,) * len(shape))
    return pl.pallas_call(
        _combine_kernel,
        grid=(t // tm,),
        in_specs=[pl.BlockSpec((TOP_K, tm), lambda i: (0, i), memory_space=pltpu.SMEM),
                  row(D_MODEL), row(TOP_K), const(g.shape), const(b.shape),
                  pl.BlockSpec(memory_space=pl.ANY)],
        out_specs=row(D_MODEL),
        out_shape=jax.ShapeDtypeStruct((t, D_MODEL), F32),
        scratch_shapes=[pltpu.VMEM((TOP_K, tm, D_MODEL), F32), pltpu.SemaphoreType.DMA],
        compiler_params=_params(("arbitrary",)),
        name="combine",
    )(dest, x2, w, g, b, yb)


def _moe_rows(t):
    n_blocks = (t * TOP_K + N_EXPERTS * (MOE_BLK - 1) + MOE_BLK - 1) // MOE_BLK
    return n_blocks * MOE_BLK


def _moe(x2, slots_init, wrt, br, wgu, wd, ln_g, ln_b):
    t = x2.shape[0]
    expert, gate, rank, counts = _router(x2, wrt, br)
    counts = counts[:, 0]
    padded = (counts + MOE_BLK - 1) // MOE_BLK * MOE_BLK
    pend = jnp.cumsum(padded)
    pstart = pend - padded
    n_blocks = _moe_rows(t) // MOE_BLK
    eid = jnp.arange(N_EXPERTS, dtype=jnp.int32)[:, None, None]
    dest = jnp.sum(jnp.where(expert[None] == eid, pstart[:, None, None], 0), axis=0) + rank
    block_start = jnp.arange(n_blocks, dtype=jnp.int32) * MOE_BLK
    block_expert = jnp.minimum(jnp.sum(block_start[:, None] >= pend[None, :], axis=1), N_EXPERTS - 1)
    block_expert = block_expert.astype(jnp.int32)
    nvalid = jnp.clip((pstart + counts)[block_expert] - block_start, 0, MOE_BLK).astype(jnp.int32)
    xs = _dispatch(dest, x2, slots_init)
    yb = _ffn(block_expert, nvalid, xs, wgu, wd)
    return _combine(dest, x2, yb, gate.T, ln_g, ln_b), yb


def _prep_weights(mix_w_in, gdn_conv_w, gdn_a_log, gdn_dt_bias, ssd_conv_w, ssd_conv_b, ssd_a_log, ssd_dt_bias,
                  ssd_d, moe_w_gate, moe_w_up, xa_w_k, xa_w_v, router_w, router_b):
    nl = mix_w_in.shape[0]
    off = [0]
    for s in (GDN_QKV, GDN_HEADS * GDN_DV, GDN_HEADS, GDN_HEADS, SSD_XBC, SSD_INNER, SSD_HEADS,
              RET_HEADS * RET_DK, RET_HEADS * RET_DK, RET_HEADS * RET_DV, RET_HEADS * RET_DV,
              N_BRANCH * D_MODEL):
        off.append(off[-1] + s)
    sl = lambda i, j=None: mix_w_in[:, :, off[i]:off[(i if j is None else j) + 1]]
    zeros = lambda n: jnp.zeros((nl, D_MODEL, n), mix_w_in.dtype)
    w_all = jnp.concatenate([
        sl(11),
        sl(0),
        sl(7, 10),
        sl(4, 5),
        sl(1),
        sl(2, 3), zeros(LANE - 2 * GDN_HEADS),
        sl(6), zeros(LANE - SSD_HEADS),
        zeros(PROJ_N - (COL_SSD_DT + 1) * LANE)], axis=-1).astype(BF16)
    wabt = jnp.swapaxes(sl(2, 3), 1, 2).astype(BF16)
    wdtt = jnp.swapaxes(sl(6), 1, 2).astype(BF16)

    def col_params(a_log, dt_bias, lane0):
        n = a_log.shape[1]
        p = jnp.zeros((nl, 2, LANE), F32)
        return p.at[:, 0, lane0:lane0 + n].set(a_log).at[:, 1, lane0:lane0 + n].set(dt_bias)

    def row_params(a_log, dt_bias):
        n = a_log.shape[1]
        p = jnp.zeros((nl, 16, 2), F32)
        return p.at[:, 0:n, 0].set(a_log).at[:, 0:n, 1].set(dt_bias)

    return dict(
        w_all=w_all, wabt=wabt, wdtt=wdtt,
        gdn_pcol=col_params(gdn_a_log, gdn_dt_bias, 0), gdn_prow=row_params(gdn_a_log, gdn_dt_bias),
        ssd_pcol=col_params(ssd_a_log, ssd_dt_bias, 0), ssd_prow=row_params(ssd_a_log, ssd_dt_bias),
        ssd_dskip=jnp.repeat(ssd_d, SSD_HEADDIM, axis=-1)[:, None, :],
        wgu=jnp.concatenate([moe_w_gate, moe_w_up], axis=-1).astype(BF16),
        wkv=jnp.concatenate([xa_w_k, xa_w_v], axis=-1).astype(BF16),
        conv_w=jnp.zeros((nl, CONV_K, PROJ_N), F32)
        .at[:, :, COL_GDN * COL_W:COL_GDN * COL_W + GDN_QKV].set(gdn_conv_w)
        .at[:, :, COL_SSD * COL_W:COL_SSD * COL_W + SSD_XBC].set(ssd_conv_w),
        conv_b=jnp.zeros((nl, 1, PROJ_N), F32)
        .at[:, 0, COL_SSD * COL_W:COL_SSD * COL_W + SSD_XBC].set(ssd_conv_b),
        wrt=router_w.T.astype(BF16), br=router_b[:, None],
    )


def kernel(x, mem, mix_w_in, gdn_conv_w, gdn_a_log, gdn_dt_bias, gdn_norm_w, ssd_conv_w, ssd_conv_b, ssd_a_log, ssd_dt_bias, ssd_d, ssd_norm_w, w_proj_gdn, w_proj_ssd, w_proj_ret, mix_w_out, ln_mix_g, ln_mix_b, xa_w_q, xa_w_k, xa_w_v, xa_w_o, ln_xa_g, ln_xa_b, router_w, router_b, moe_w_gate, moe_w_up, moe_w_down, ln_moe_g, ln_moe_b):
    bsz, seq, d = x.shape
    t = bsz * seq
    p = _prep_weights(mix_w_in, gdn_conv_w, gdn_a_log, gdn_dt_bias, ssd_conv_w, ssd_conv_b, ssd_a_log,
                      ssd_dt_bias, ssd_d, moe_w_gate, moe_w_up, xa_w_k, xa_w_v, router_w, router_b)
    bf = lambda w: w.astype(BF16)
    wpg, wps, wpr, wout = bf(w_proj_gdn), bf(w_proj_ssd), bf(w_proj_ret), bf(mix_w_out)
    wq, wo, wd = bf(xa_w_q), bf(xa_w_o), bf(moe_w_down)
    vec = lambda v, l: v[l][None, :]

    pos = jnp.arange(seq, dtype=F32)
    inv_freq = ROPE_BASE ** (-jnp.arange(0, RET_DK, 2, dtype=F32) / RET_DK)
    ang = pos[:, None] * inv_freq[None, :]
    cos, sin = jnp.cos(ang), jnp.sin(ang)
    cos2 = jnp.concatenate([cos, cos], axis=-1)
    sin2 = jnp.concatenate([-sin, sin], axis=-1)
    mem2 = mem.reshape(bsz * mem.shape[1], d)

    x2 = x.reshape(t, d)
    slots = jnp.zeros((_moe_rows(t), d), F32)
    for l in range(DEPTH):
        h2 = _proj(x2, p["w_all"][l], p["conv_w"][l], p["conv_b"][l], seq)
        x3, h3 = x2.reshape(bsz, seq, d), h2.reshape(bsz, seq, PROJ_N)
        yg = _gdn(x3, h3, p["wabt"][l], p["gdn_pcol"][l], p["gdn_prow"][l], vec(gdn_norm_w, l))
        ys = _ssd(x3, h3, p["wdtt"][l], p["ssd_pcol"][l], p["ssd_prow"][l], p["ssd_dskip"][l],
                  vec(ssd_norm_w, l))
        yr = _ret(h3, cos2, sin2)
        x2 = _merge(x2, h2, yg.reshape(t, d), ys.reshape(t, d), yr.reshape(t, d), wpg[l], wps[l], wpr[l],
                    wout[l], vec(ln_mix_g, l), vec(ln_mix_b, l))
        kv = _matmul(mem2, p["wkv"][l], BF16, 512, 1024)
        x2 = _xattn(x2.reshape(bsz, seq, d), kv.reshape(bsz, mem.shape[1], 2 * d), wq[l], wo[l],
                    vec(ln_xa_g, l), vec(ln_xa_b, l)).reshape(t, d)
        x2, slots = _moe(x2, slots, p["wrt"], p["br"], p["wgu"][l], wd[l], vec(ln_moe_g, l), vec(ln_moe_b, l))
    return x2.reshape(bsz, seq, d)
```
